```python
import math
import jax
import jax.numpy as jnp
from jax import lax
import numpy as np

D_MODEL = 2048
BATCH = 16
SEQ = 2048
DEPTH = 2

GRID_W = 64
CTX_LEN = 256
N_BRANCH = 4
MIX_W = D_MODEL // 2
D_FF = ((8 * D_MODEL // 3 + 255) // 256) * 256
FFN_HALF = 0.5
EPS = 1e-6
ROPE_THETA = 10000.0
Q_BLOCK = 128
CONV_K = 4
N_MOD = 9

SSM_HEAD_DIM = 64
SSM_HEADS = MIX_W // SSM_HEAD_DIM
SSM_GROUPS = 2
SSM_STATE = 128
SSM_CHUNK = 128
SSM_CONV_DIM = MIX_W + 2 * SSM_GROUPS * SSM_STATE

DIFF_HEAD_DIM = 64
DIFF_HEADS = MIX_W // (2 * DIFF_HEAD_DIM)

GQA_HEAD_DIM = 128
GQA_HEADS = MIX_W // GQA_HEAD_DIM
GQA_KV_HEADS = GQA_HEADS // 4

LRU_BLOCKS = 16
LRU_BLOCK_DIM = MIX_W // LRU_BLOCKS
LRU_C = 8.0

IN_SIZES = (MIX_W, SSM_CONV_DIM, 2 * SSM_HEADS,
            2 * DIFF_HEADS * DIFF_HEAD_DIM, 2 * DIFF_HEADS * DIFF_HEAD_DIM, 2 * DIFF_HEADS * DIFF_HEAD_DIM,
            GQA_HEADS * GQA_HEAD_DIM, GQA_KV_HEADS * GQA_HEAD_DIM, GQA_KV_HEADS * GQA_HEAD_DIM,
            MIX_W, MIX_W)
D_IN = sum(IN_SIZES)

kernel_name = 'hybrid_ssd_diffattn_gqa_rglru_flow_block'


def _split(t, sizes):
    return jnp.split(t, np.cumsum(sizes)[:-1].tolist(), axis=-1)


def rmsnorm(x, g):
    xf = x.astype(jnp.float32)
    y = xf * lax.rsqrt(jnp.mean(xf * xf, axis=-1, keepdims=True) + EPS)
    return (y * g.astype(jnp.float32)).astype(x.dtype)


def modulate(n, shift, scale):
    return n * (1.0 + scale) + shift


def swiglu(n, wg, wu, wd):
    return (jax.nn.silu(n @ wg) * (n @ wu)) @ wd


def ffn_sublayer(x, shift, scale, gate, g_pre, g_post, wg, wu, wd):
    y = swiglu(modulate(rmsnorm(x, g_pre), shift, scale), wg, wu, wd)
    return x + FFN_HALF * gate * rmsnorm(y, g_post)


def axial_rope(row, col, head_dim):
    half = head_dim // 2
    inv = ROPE_THETA ** (-(jnp.arange(0, half, 2, dtype=jnp.float32) / half))
    ang_r = row[:, None] * inv[None, :]
    ang_c = col[:, None] * inv[None, :]
    return (jnp.cos(ang_r), jnp.sin(ang_r), jnp.cos(ang_c), jnp.sin(ang_c))


def _rotate(v, cos, sin):
    shape = (cos.shape[0],) + (1,) * (v.ndim - 3) + (cos.shape[1],)
    cos = cos.reshape(shape)
    sin = sin.reshape(shape)
    n = v.shape[-1] // 2
    v1, v2 = v[..., :n], v[..., n:]
    return jnp.concatenate([v1 * cos - v2 * sin, v2 * cos + v1 * sin], axis=-1).astype(v.dtype)


def apply_axial_rope(v, rope):
    cos_r, sin_r, cos_c, sin_c = rope
    half = v.shape[-1] // 2
    return jnp.concatenate([_rotate(v[..., :half], cos_r, sin_r),
                            _rotate(v[..., half:], cos_c, sin_c)], axis=-1)


def dwconv_centred(x, w, b):
    ch = x.shape[-1]
    k = w.shape[0]
    y = lax.conv_general_dilated(x, w[:, None, :].astype(x.dtype), window_strides=(1,),
                                 padding=[((k - 1) // 2, k // 2)],
                                 dimension_numbers=('NWC', 'WIO', 'NWC'),
                                 feature_group_count=ch)
    return y + b.astype(x.dtype)


def sweep_query_blocks(fn, q):
    b, t = q.shape[:2]
    nb = t // Q_BLOCK
    qb = jnp.moveaxis(q.reshape((b, nb, Q_BLOCK) + q.shape[2:]), 1, 0)
    out = lax.map(fn, qb)
    out = jnp.moveaxis(out, 0, 1)
    return out.reshape((b, t) + out.shape[3:])


def gqa_attend(q, k, v):
    b, tq, hq, d = q.shape
    hkv = k.shape[2]
    qg = q.reshape(b, tq, hkv, hq // hkv, d)
    s = jnp.einsum('bqhgd,bkhd->bhgqk', qg.astype(jnp.float32), k.astype(jnp.float32)) * (d ** -0.5)
    p = jax.nn.softmax(s, axis=-1)
    o = jnp.einsum('bhgqk,bkhd->bqhgd', p.astype(v.dtype), v)
    return o.reshape(b, tq, hq, d)


def diff_attend(q, k, v, lam):
    d = q.shape[-1]
    s = jnp.einsum('bqhmd,bkhmd->bhmqk', q.astype(jnp.float32), k.astype(jnp.float32)) * (d ** -0.5)
    p = jax.nn.softmax(s, axis=-1)
    pd = p[:, :, 0] - lam * p[:, :, 1]
    return jnp.einsum('bhqk,bkhe->bqhe', pd.astype(v.dtype), v)


def ssd_chunked(x, dt, a, bm, cm, h0, want_y):
    b, t, nh, p = x.shape
    g, n = bm.shape[2], bm.shape[3]
    hg = nh // g
    q = SSM_CHUNK
    nc = t // q
    xc = x.reshape(b, nc, q, g, hg, p).astype(jnp.float32)
    dtc = dt.reshape(b, nc, q, g, hg)
    bc = bm.reshape(b, nc, q, g, n).astype(jnp.float32)
    cc = cm.reshape(b, nc, q, g, n).astype(jnp.float32)
    cs = jnp.cumsum(dtc * a.reshape(g, hg), axis=2)
    decay_to_end = jnp.exp(cs[:, :, -1:] - cs)
    states = jnp.einsum('bclgn,bclgh,bclghp->bcghpn', bc, decay_to_end * dtc, xc)
    chunk_decay = jnp.exp(cs[:, :, -1])

    def step(h, inp):
        dec, st = inp
        return dec[..., None, None] * h + st, h

    h_last, h_starts = lax.scan(step, h0.reshape(b, g, hg, p, n).astype(jnp.float32),
                                (jnp.moveaxis(chunk_decay, 1, 0), jnp.moveaxis(states, 1, 0)))
    h_last = h_last.reshape(b, nh, p, n)
    if not want_y:
        return None, h_last
    h_starts = jnp.moveaxis(h_starts, 0, 1)
    seg = cs[:, :, :, None] - cs[:, :, None, :]
    mask = jnp.tril(jnp.ones((q, q), dtype=bool))[:, :, None, None]
    lmat = jnp.exp(jnp.where(mask, seg, -jnp.inf))
    cb = jnp.einsum('bclgn,bcsgn->bclsg', cc, bc)
    y_diag = jnp.einsum('bclsg,bclsgh,bcsgh,bcsghp->bclghp', cb, lmat, dtc, xc)
    y_off = jnp.einsum('bclgn,bcghpn,bclgh->bclghp', cc, h_starts, jnp.exp(cs))
    return (y_diag + y_off).reshape(b, t, nh, p), h_last


def ssd_bidir(xs, bm, cm, dt, a, h0_f, h0_b, want_y):
    flip = lambda t: jnp.flip(t, axis=1)
    y_f, h_f = ssd_chunked(xs, dt[:, :, 0], a[0], bm, cm, h0_f, want_y)
    y_b, h_b = ssd_chunked(flip(xs), flip(dt[:, :, 1]), a[1], flip(bm), flip(cm), h0_b, want_y)
    y = y_f + flip(y_b) if want_y else None
    return y, h_f, h_b


def ssm_prep(xbc, dt_raw, conv_w, conv_b, dt_bias):
    b, t = xbc.shape[:2]
    xbc = jax.nn.silu(dwconv_centred(xbc, conv_w, conv_b))
    xs, bm, cm = _split(xbc, (MIX_W, SSM_GROUPS * SSM_STATE, SSM_GROUPS * SSM_STATE))
    xs = xs.reshape(b, t, SSM_HEADS, SSM_HEAD_DIM)
    bm = bm.reshape(b, t, SSM_GROUPS, SSM_STATE)
    cm = cm.reshape(b, t, SSM_GROUPS, SSM_STATE)
    dt = jax.nn.softplus(dt_raw.astype(jnp.float32).reshape(b, t, 2, SSM_HEADS) + dt_bias.astype(jnp.float32))
    return xs, bm, cm, dt


def ssm_out(y, xs, z, d_skip, g_norm):
    b, t = z.shape[:2]
    y = y + d_skip.astype(jnp.float32)[:, None] * xs.astype(jnp.float32)
    y = y.reshape(b, t, MIX_W) * jax.nn.silu(z.astype(jnp.float32))
    y = rmsnorm(y.reshape(b, t, SSM_GROUPS, MIX_W // SSM_GROUPS), g_norm.reshape(SSM_GROUPS, -1))
    return y.reshape(b, t, MIX_W).astype(z.dtype)


def ssd_mixer(lat, ctx, conv_w, conv_b, a_log, dt_bias, d_skip, g_norm, ctx_out):
    a = -jnp.exp(a_log.astype(jnp.float32))
    z_c, xbc_c, dt_c = ctx
    xs_c, b_c, c_c, d_c = ssm_prep(xbc_c, dt_c, conv_w, conv_b, dt_bias)
    h0 = jnp.zeros((xs_c.shape[0], SSM_HEADS, SSM_HEAD_DIM, SSM_STATE), jnp.float32)
    y_c, h_f, h_b = ssd_bidir(xs_c, b_c, c_c, d_c, a, h0, h0, ctx_out)
    z_l, xbc_l, dt_l = lat
    xs_l, b_l, c_l, d_l = ssm_prep(xbc_l, dt_l, conv_w, conv_b, dt_bias)
    y_l, _, _ = ssd_bidir(xs_l, b_l, c_l, d_l, a, h_f, h_b, True)
    out_l = ssm_out(y_l, xs_l, z_l, d_skip, g_norm)
    out_c = ssm_out(y_c, xs_c, z_c, d_skip, g_norm) if ctx_out else None
    return out_l, out_c


def diff_attn_mixer(lat, ctx, lambda_q, lambda_k, g_norm, lambda_init, rope, ctx_out):
    def heads(q, k, v):
        b, t = q.shape[:2]
        return (q.reshape(b, t, DIFF_HEADS, 2, DIFF_HEAD_DIM),
                k.reshape(b, t, DIFF_HEADS, 2, DIFF_HEAD_DIM),
                v.reshape(b, t, DIFF_HEADS, 2 * DIFF_HEAD_DIM))
    q_l, k_l, v_l = heads(*lat)
    q_c, k_c, v_c = heads(*ctx)
    lq = lambda_q.astype(jnp.float32)
    lk = lambda_k.astype(jnp.float32)
    lam = jnp.exp(jnp.sum(lq[0] * lk[0])) - jnp.exp(jnp.sum(lq[1] * lk[1])) + lambda_init
    q_l = apply_axial_rope(q_l, rope)
    k_l = apply_axial_rope(k_l, rope)
    k_all = jnp.concatenate([k_l, k_c], axis=1)
    v_all = jnp.concatenate([v_l, v_c], axis=1)
    o_l = sweep_query_blocks(lambda qb: diff_attend(qb, k_all, v_all, lam), q_l)

    def finish(o):
        b, t = o.shape[:2]
        return (rmsnorm(o, g_norm) * (1.0 - lambda_init)).reshape(b, t, MIX_W)

    out_c = finish(diff_attend(q_c, k_c, v_c, lam)) if ctx_out else None
    return finish(o_l), out_c


def gqa_mixer(lat, ctx, g_q, g_k, rope, ctx_out):
    def heads(q, k, v):
        b, t = q.shape[:2]
        q = rmsnorm(q.reshape(b, t, GQA_HEADS, GQA_HEAD_DIM), g_q)
        k = rmsnorm(k.reshape(b, t, GQA_KV_HEADS, GQA_HEAD_DIM), g_k)
        return q, k, v.reshape(b, t, GQA_KV_HEADS, GQA_HEAD_DIM)
    q_l, k_l, v_l = heads(*lat)
    q_c, k_c, v_c = heads(*ctx)
    q_l = apply_axial_rope(q_l, rope)
    k_l = apply_axial_rope(k_l, rope)
    k_all = jnp.concatenate([k_l, k_c], axis=1)
    v_all = jnp.concatenate([v_l, v_c], axis=1)
    o_l = sweep_query_blocks(lambda qb: gqa_attend(qb, k_all, v_all), q_l)
    b, t = o_l.shape[:2]
    out_l = o_l.reshape(b, t, MIX_W)
    out_c = gqa_attend(q_c, k_c, v_c).reshape(b, -1, MIX_W) if ctx_out else None
    return out_l, out_c


def blockdiag(x, w):
    b, t = x.shape[:2]
    y = jnp.einsum('btnd,nde->btne', x.reshape(b, t, LRU_BLOCKS, LRU_BLOCK_DIM), w)
    return y.reshape(b, t, MIX_W)


def rglru_coeffs(xr, w_rg, b_rg, w_ig, b_ig, lam):
    r = jax.nn.sigmoid(blockdiag(xr, w_rg) + b_rg).astype(jnp.float32)
    i = jax.nn.sigmoid(blockdiag(xr, w_ig) + b_ig).astype(jnp.float32)
    log_a = -LRU_C * r * jax.nn.softplus(-lam.astype(jnp.float32))
    a = jnp.exp(log_a)
    bx = jnp.sqrt(-jnp.expm1(2.0 * log_a)) * (i * xr.astype(jnp.float32))
    return a, bx


def linear_scan(a, bx, h0):
    def combine(lft, rgt):
        return (lft[0] * rgt[0], rgt[0] * lft[1] + rgt[1])
    a_cum, h = lax.associative_scan(combine, (a, bx), axis=1)
    h = h + a_cum * h0[:, None]
    return h, h[:, -1]


def rglru_bidir(xr, w_rg, b_rg, w_ig, b_ig, lam, h0_f, h0_b):
    flip = lambda t: jnp.flip(t, axis=1)
    a_f, bx_f = rglru_coeffs(xr, w_rg[0], b_rg[0], w_ig[0], b_ig[0], lam[0])
    h_f, hT_f = linear_scan(a_f, bx_f, h0_f)
    a_b, bx_b = rglru_coeffs(xr, w_rg[1], b_rg[1], w_ig[1], b_ig[1], lam[1])
    h_b, hT_b = linear_scan(flip(a_b), flip(bx_b), h0_b)
    return h_f + flip(h_b), hT_f, hT_b


def rglru_mixer(lat, ctx, conv_w, conv_b, w_rg, b_rg, w_ig, b_ig, lam, ctx_out):
    g_c, x_c = ctx
    xr_c = dwconv_centred(x_c, conv_w, conv_b)
    h0 = jnp.zeros((xr_c.shape[0], MIX_W), jnp.float32)
    h_c, hT_f, hT_b = rglru_bidir(xr_c, w_rg, b_rg, w_ig, b_ig, lam, h0, h0)
    g_l, x_l = lat
    xr_l = dwconv_centred(x_l, conv_w, conv_b)
    h_l, _, _ = rglru_bidir(xr_l, w_rg, b_rg, w_ig, b_ig, lam, hT_f, hT_b)
    out_l = (jax.nn.gelu(g_l) * h_l).astype(x_l.dtype)
    out_c = (jax.nn.gelu(g_c) * h_c).astype(x_c.dtype) if ctx_out else None
    return out_l, out_c


def token_mixer(n_lat, n_ctx, w_in, conv_w_ssm, conv_b_ssm, a_log, dt_bias, d_skip, g_ssm_norm,
                lambda_q, lambda_k, g_diff_norm, lambda_init, g_q_norm, g_k_norm,
                conv_w_lru, conv_b_lru, w_rg, b_rg, w_ig, b_ig, lru_lambda,
                w_branch, w_gate, w_out, rope_diff, rope_gqa, ctx_out):
    p_l = _split(n_lat @ w_in, IN_SIZES)
    p_c = _split(n_ctx @ w_in, IN_SIZES)
    ssm_l, ssm_c = ssd_mixer(p_l[0:3], p_c[0:3], conv_w_ssm, conv_b_ssm, a_log, dt_bias, d_skip,
                             g_ssm_norm, ctx_out)
    dif_l, dif_c = diff_attn_mixer(p_l[3:6], p_c[3:6], lambda_q, lambda_k, g_diff_norm, lambda_init,
                                   rope_diff, ctx_out)
    gqa_l, gqa_c = gqa_mixer(p_l[6:9], p_c[6:9], g_q_norm, g_k_norm, rope_gqa, ctx_out)
    lru_l, lru_c = rglru_mixer(p_l[9:11], p_c[9:11], conv_w_lru, conv_b_lru, w_rg, b_rg, w_ig, b_ig,
                               lru_lambda, ctx_out)

    def merge(n, branches):
        y = 0.0
        for i, br in enumerate(branches):
            y = y + jax.nn.sigmoid(n @ w_gate[i]) * (br @ w_branch[i])
        return y @ w_out

    y_l = merge(n_lat, (ssm_l, dif_l, gqa_l, lru_l))
    y_c = merge(n_ctx, (ssm_c, dif_c, gqa_c, lru_c)) if ctx_out else None
    return y_l, y_c


def setup_inputs(seed: int = 0) -> dict:
    key = jax.random.key(seed)
    ks = jax.random.split(key, 40)
    f32 = jnp.float32

    def nrm(k, shape, scale):
        return jax.random.normal(k, shape, f32) * scale

    def gain(k, shape):
        return 1.0 + 0.05 * jax.random.normal(k, shape, f32)

    dt0 = jnp.exp(jax.random.uniform(ks[11], (DEPTH, 2, SSM_HEADS), f32, math.log(1e-3), math.log(1e-1)))
    a_c = jax.random.uniform(ks[24], (DEPTH, 2, MIX_W), f32, 0.9, 0.999)
    a_base = a_c ** (1.0 / LRU_C)
    return {
        'x': nrm(ks[0], (BATCH, SEQ, D_MODEL), 1.0),
        'c': nrm(ks[1], (BATCH, D_MODEL), 1.0),
        'ctx': nrm(ks[2], (BATCH, CTX_LEN, D_MODEL), 1.0),
        'c_ctx': nrm(ks[3], (D_MODEL,), 1.0),
        'w_ada': nrm(ks[4], (DEPTH, D_MODEL, N_MOD * D_MODEL), 0.5 * D_MODEL ** -0.5),
        'b_ada': nrm(ks[5], (DEPTH, N_MOD * D_MODEL), 0.02),
        'g_pre': gain(ks[6], (DEPTH, 3, D_MODEL)),
        'g_post': gain(ks[7], (DEPTH, 3, D_MODEL)),
        'w_ffn_gate': nrm(ks[8], (DEPTH, 2, D_MODEL, D_FF), D_MODEL ** -0.5),
        'w_ffn_up': nrm(ks[9], (DEPTH, 2, D_MODEL, D_FF), D_MODEL ** -0.5),
        'w_ffn_down': nrm(ks[10], (DEPTH, 2, D_FF, D_MODEL), D_FF ** -0.5),
        'w_in': nrm(ks[12], (DEPTH, D_MODEL, D_IN), D_MODEL ** -0.5),
        'conv_w_ssm': nrm(ks[13], (DEPTH, CONV_K, SSM_CONV_DIM), CONV_K ** -0.5),
        'conv_b_ssm': nrm(ks[14], (DEPTH, SSM_CONV_DIM), 0.02),
        'a_log': jnp.log(jax.random.uniform(ks[15], (DEPTH, 2, SSM_HEADS), f32, 1.0, 16.0)),
        'dt_bias': dt0 + jnp.log(-jnp.expm1(-dt0)),
        'd_skip': gain(ks[16], (DEPTH, SSM_HEADS)),
        'g_ssm_norm': gain(ks[17], (DEPTH, MIX_W)),
        'lambda_q': nrm(ks[18], (DEPTH, 2, DIFF_HEAD_DIM), 0.1),
        'lambda_k': nrm(ks[19], (DEPTH, 2, DIFF_HEAD_DIM), 0.1),
        'g_diff_norm': gain(ks[20], (DEPTH, 2 * DIFF_HEAD_DIM)),
        'g_q_norm': gain(ks[21], (DEPTH, GQA_HEAD_DIM)),
        'g_k_norm': gain(ks[22], (DEPTH, GQA_HEAD_DIM)),
        'conv_w_lru': nrm(ks[23], (DEPTH, CONV_K, MIX_W), CONV_K ** -0.5),
        'conv_b_lru': nrm(ks[25], (DEPTH, MIX_W), 0.02),
        'w_rg': nrm(ks[26], (DEPTH, 2, LRU_BLOCKS, LRU_BLOCK_DIM, LRU_BLOCK_DIM), LRU_BLOCK_DIM ** -0.5),
        'b_rg': nrm(ks[27], (DEPTH, 2, MIX_W), 0.02),
        'w_ig': nrm(ks[28], (DEPTH, 2, LRU_BLOCKS, LRU_BLOCK_DIM, LRU_BLOCK_DIM), LRU_BLOCK_DIM ** -0.5),
        'b_ig': nrm(ks[29], (DEPTH, 2, MIX_W), 0.02),
        'lru_lambda': jnp.log(a_base) - jnp.log1p(-a_base),
        'w_branch': nrm(ks[30], (DEPTH, N_BRANCH, MIX_W, D_MODEL), MIX_W ** -0.5),
        'w_gate': nrm(ks[31], (DEPTH, N_BRANCH, D_MODEL, D_MODEL), D_MODEL ** -0.5),
        'w_out': nrm(ks[32], (DEPTH, D_MODEL, D_MODEL), D_MODEL ** -0.5),
    }


def reference(x, c, ctx, c_ctx, w_ada, b_ada, g_pre, g_post, w_ffn_gate, w_ffn_up, w_ffn_down,
              w_in, conv_w_ssm, conv_b_ssm, a_log, dt_bias, d_skip, g_ssm_norm,
              lambda_q, lambda_k, g_diff_norm, g_q_norm, g_k_norm,
              conv_w_lru, conv_b_lru, w_rg, b_rg, w_ig, b_ig, lru_lambda,
              w_branch, w_gate, w_out):
    bsz, n_tok = x.shape[:2]
    rows = n_tok // GRID_W
    row = jnp.repeat(jnp.arange(rows, dtype=jnp.int32), GRID_W).astype(jnp.float32)
    col = (jnp.arange(n_tok, dtype=jnp.int32) % GRID_W).astype(jnp.float32)
    rope_diff = axial_rope(row, col, DIFF_HEAD_DIM)
    rope_gqa = axial_rope(row, col, GQA_HEAD_DIM)
    silu_c = jax.nn.silu(c)
    silu_cc = jax.nn.silu(c_ctx)
    h, hc = x, ctx
    for l in range(DEPTH):
        ctx_out = l < DEPTH - 1
        lambda_init = 0.8 - 0.6 * math.exp(-0.3 * l)
        mod_l = (silu_c @ w_ada[l] + b_ada[l]).reshape(bsz, N_MOD, 1, D_MODEL)
        mod_c = (silu_cc @ w_ada[l] + b_ada[l]).reshape(N_MOD, 1, D_MODEL)
        h = ffn_sublayer(h, mod_l[:, 0], mod_l[:, 1], mod_l[:, 2], g_pre[l, 0], g_post[l, 0],
                         w_ffn_gate[l, 0], w_ffn_up[l, 0], w_ffn_down[l, 0])
        hc = ffn_sublayer(hc, mod_c[0], mod_c[1], mod_c[2], g_pre[l, 0], g_post[l, 0],
                          w_ffn_gate[l, 0], w_ffn_up[l, 0], w_ffn_down[l, 0])
        n_l = modulate(rmsnorm(h, g_pre[l, 1]), mod_l[:, 3], mod_l[:, 4])
        n_c = modulate(rmsnorm(hc, g_pre[l, 1]), mod_c[3], mod_c[4])
        y_l, y_c = token_mixer(n_l, n_c, w_in[l], conv_w_ssm[l], conv_b_ssm[l], a_log[l], dt_bias[l],
                               d_skip[l], g_ssm_norm[l], lambda_q[l], lambda_k[l], g_diff_norm[l],
                               lambda_init, g_q_norm[l], g_k_norm[l], conv_w_lru[l], conv_b_lru[l],
                               w_rg[l], b_rg[l], w_ig[l], b_ig[l], lru_lambda[l],
                               w_branch[l], w_gate[l], w_out[l], rope_diff, rope_gqa, ctx_out)
        h = h + mod_l[:, 5] * rmsnorm(y_l, g_post[l, 1])
        h = ffn_sublayer(h, mod_l[:, 6], mod_l[:, 7], mod_l[:, 8], g_pre[l, 2], g_post[l, 2],
                         w_ffn_gate[l, 1], w_ffn_up[l, 1], w_ffn_down[l, 1])
        if ctx_out:
            hc = hc + mod_c[5] * rmsnorm(y_c, g_post[l, 1])
            hc = ffn_sublayer(hc, mod_c[6], mod_c[7], mod_c[8], g_pre[l, 2], g_post[l, 2],
                              w_ffn_gate[l, 1], w_ffn_up[l, 1], w_ffn_down[l, 1])
    return h
```

```python
import functools
import math

import jax
import jax.numpy as jnp
from jax import lax
from jax.experimental import pallas as pl
from jax.experimental.pallas import tpu as pltpu

F32 = jnp.float32
BF16 = jnp.bfloat16

GRID_W = 64
N_MOD = 9
FFN_HALF = 0.5
EPS = 1e-6
ROPE_THETA = 10000.0
CONV_K = 4
SSM_HEAD_DIM = 64
SSM_GROUPS = 2
SSM_STATE = 128
SSM_CHUNK = 128
DIFF_HEAD_DIM = 64
GQA_HEAD_DIM = 128
GQA_GROUP = 4
LRU_BLOCK_DIM = 64
LRU_C = 8.0

LANE = 128
SUBLANE = 8
V7X_VMEM_BYTES = 64 * 1024 * 1024
VMEM_LIMIT = V7X_VMEM_BYTES - 6 * 1024 * 1024


def _cparams(sem):
    return pltpu.CompilerParams(dimension_semantics=sem, vmem_limit_bytes=VMEM_LIMIT)


def _dot(a, b):
    return jnp.dot(a, b, preferred_element_type=F32)


def _dot_nt(a, b):
    return lax.dot_general(a, b, (((1,), (1,)), ((), ())), preferred_element_type=F32)


def _dot_tn(a, b):
    return lax.dot_general(a, b, (((0,), (0,)), ((), ())), preferred_element_type=F32)


def _rms(x, g):
    return x * lax.rsqrt(jnp.mean(x * x, axis=-1, keepdims=True) + EPS) * g


def _split_dot(a_f32, b_bf16):
    hi = a_f32.astype(BF16)
    lo = (a_f32 - hi.astype(F32)).astype(BF16)
    return _dot(hi, b_bf16) + _dot(lo, b_bf16)


def _adaln_kernel(c_ref, w_ref, b_ref, o_ref):
    x = c_ref[...]
    x = (x * jax.nn.sigmoid(x)).astype(BF16)
    o_ref[0] = _dot(x, w_ref[0].astype(BF16)) + b_ref[0]


def adaln(cc, w_ada, b_ada, *, tn=1024):
    depth, d, n = w_ada.shape
    r = cc.shape[0]
    return pl.pallas_call(
        _adaln_kernel,
        out_shape=jax.ShapeDtypeStruct((depth, r, n), F32),
        grid=(depth, n // tn),
        in_specs=[
            pl.BlockSpec((r, d), lambda l, j: (0, 0)),
            pl.BlockSpec((1, d, tn), lambda l, j: (l, 0, j)),
            pl.BlockSpec((1, 1, tn), lambda l, j: (l, 0, j)),
        ],
        out_specs=pl.BlockSpec((1, r, tn), lambda l, j: (l, 0, j)),
        compiler_params=_cparams(("parallel", "parallel")),
        name="adaln",
    )(cc, w_ada, b_ada.reshape(depth, 1, n))


def _mod_rows(modl_ref, modc_ref, j, row0, tm, t_lat, mixed):
    ml = modl_ref[0, j:j + 1, :]
    if not mixed:
        return ml
    rows = row0 + lax.broadcasted_iota(jnp.int32, (tm, 1), 0)
    return jnp.where(rows >= t_lat, modc_ref[j:j + 1, :], ml)


def _norm_mod(x, g, modl_ref, modc_ref, base, row0, tm, t_lat, mixed):
    shift = _mod_rows(modl_ref, modc_ref, base, row0, tm, t_lat, mixed)
    scale = _mod_rows(modl_ref, modc_ref, base + 1, row0, tm, t_lat, mixed)
    return _rms(x, g) * (1.0 + scale) + shift


ROW_CHUNK = 64
COL_CHUNK = 512


def _prologue(h_ref, n_ref, o_ref, gpre_ref, modl_ref, modc_ref, base, row0, tm, t_lat, mixed):
    g = gpre_ref[...]

    def body(r, carry):
        rs = pl.multiple_of(r * ROW_CHUNK, ROW_CHUNK)
        sl = pl.ds(rs, ROW_CHUNK)
        n = _norm_mod(h_ref[0, sl, :], g, modl_ref, modc_ref, base, row0 + rs, ROW_CHUNK, t_lat, mixed)
        n_ref[sl, :] = n.astype(BF16)
        if o_ref is not None:
            o_ref[0, sl, :] = jnp.zeros((ROW_CHUNK, o_ref.shape[-1]), F32)
        return carry

    lax.fori_loop(0, tm // ROW_CHUNK, body, 0)


def _accumulate(o_ref, a, w_ref):
    for c0 in range(0, o_ref.shape[-1], COL_CHUNK):
        o_ref[0, :, c0:c0 + COL_CHUNK] += _dot(a, w_ref[:, c0:c0 + COL_CHUNK])


def _epilogue(h_ref, o_ref, gpost_ref, modl_ref, modc_ref, gate_idx, factor, row0, tm, t_lat, mixed):
    g = gpost_ref[...]

    def body(r, carry):
        rs = pl.multiple_of(r * ROW_CHUNK, ROW_CHUNK)
        sl = pl.ds(rs, ROW_CHUNK)
        gate = _mod_rows(modl_ref, modc_ref, gate_idx, row0 + rs, ROW_CHUNK, t_lat, mixed)
        o_ref[0, sl, :] = h_ref[0, sl, :] + factor * gate * _rms(o_ref[0, sl, :], g)
        return carry

    lax.fori_loop(0, tm // ROW_CHUNK, body, 0)


def _ffn_kernel(h_ref, modl_ref, modc_ref, gpre_ref, gpost_ref, wg_ref, wu_ref, wd_ref,
                o_ref, n_ref, *, tm, t_lat, base, nk, mixed):
    i = pl.program_id(1)
    k = pl.program_id(2)
    row0 = i * tm

    @pl.when(k == 0)
    def _():
        _prologue(h_ref, n_ref, o_ref, gpre_ref, modl_ref, modc_ref, base, row0, tm, t_lat, mixed)

    n = n_ref[...]
    g = _dot(n, wg_ref[...])
    u = _dot(n, wu_ref[...])
    a = (g * jax.nn.sigmoid(g) * u).astype(BF16)
    _accumulate(o_ref, a, wd_ref)

    @pl.when(k == nk - 1)
    def _():
        _epilogue(h_ref, o_ref, gpost_ref, modl_ref, modc_ref, base + 2, FFN_HALF, row0, tm, t_lat, mixed)


def ffn_sublayer(h, modl, modc, g_pre, g_post, wg, wu, wd, *, base, t_lat, rows, tm, tf=512):
    bsz, _, d = h.shape
    dff = wg.shape[1]
    nk = dff // tf
    mixed = rows > t_lat
    kern = functools.partial(_ffn_kernel, tm=tm, t_lat=t_lat, base=base, nk=nk, mixed=mixed)
    return pl.pallas_call(
        kern,
        out_shape=jax.ShapeDtypeStruct((bsz, rows, d), F32),
        grid=(bsz, rows // tm, nk),
        in_specs=[
            pl.BlockSpec((1, tm, d), lambda b, i, k: (b, i, 0)),
            pl.BlockSpec((1, N_MOD, d), lambda b, i, k: (b, 0, 0)),
            pl.BlockSpec((N_MOD, d), lambda b, i, k: (0, 0)),
            pl.BlockSpec((1, d), lambda b, i, k: (0, 0)),
            pl.BlockSpec((1, d), lambda b, i, k: (0, 0)),
            pl.BlockSpec((d, tf), lambda b, i, k: (0, k)),
            pl.BlockSpec((d, tf), lambda b, i, k: (0, k)),
            pl.BlockSpec((tf, d), lambda b, i, k: (k, 0)),
        ],
        out_specs=pl.BlockSpec((1, tm, d), lambda b, i, k: (b, i, 0)),
        scratch_shapes=[pltpu.VMEM((tm, d), BF16)],
        compiler_params=_cparams(("parallel", "parallel", "arbitrary")),
        name="ffn_sublayer",
    )(h, modl, modc, g_pre.reshape(1, d), g_post.reshape(1, d), wg, wu, wd)


def _proj_kernel(h_ref, modl_ref, modc_ref, gpre_ref, w_ref, wdt_ref, o_ref, odt_ref, n_ref,
                 *, tm, t_lat, base):
    i = pl.program_id(1)
    j = pl.program_id(2)

    @pl.when(j == 0)
    def _():
        _prologue(h_ref, n_ref, None, gpre_ref, modl_ref, modc_ref, base, i * tm, tm, t_lat, True)
        odt_ref[0] = _dot(n_ref[...], wdt_ref[...])

    o_ref[0] = _dot(n_ref[...], w_ref[...])


def in_projection(h, modl, modc, g_pre, w_main, w_dt, *, base, t_lat, tm, tn=512):
    bsz, s, d = h.shape
    ncol = w_main.shape[1]
    ndt = w_dt.shape[1]
    kern = functools.partial(_proj_kernel, tm=tm, t_lat=t_lat, base=base)
    return pl.pallas_call(
        kern,
        out_shape=(jax.ShapeDtypeStruct((bsz, s, ncol), F32),
                   jax.ShapeDtypeStruct((bsz, s, ndt), F32)),
        grid=(bsz, s // tm, ncol // tn),
        in_specs=[
            pl.BlockSpec((1, tm, d), lambda b, i, j: (b, i, 0)),
            pl.BlockSpec((1, N_MOD, d), lambda b, i, j: (b, 0, 0)),
            pl.BlockSpec((N_MOD, d), lambda b, i, j: (0, 0)),
            pl.BlockSpec((1, d), lambda b, i, j: (0, 0)),
            pl.BlockSpec((d, tn), lambda b, i, j: (0, j)),
            pl.BlockSpec((d, ndt), lambda b, i, j: (0, 0)),
        ],
        out_specs=(pl.BlockSpec((1, tm, tn), lambda b, i, j: (b, i, j)),
                   pl.BlockSpec((1, tm, ndt), lambda b, i, j: (b, i, 0))),
        scratch_shapes=[pltpu.VMEM((tm, d), BF16)],
        compiler_params=_cparams(("parallel", "parallel", "arbitrary")),
        name="in_projection",
    )(h, modl, modc, g_pre.reshape(1, d), w_main, w_dt)


def _merge_kernel(h_ref, modl_ref, modc_ref, gpre_ref, gpost_ref, b0_ref, b1_ref, b2_ref, b3_ref,
                  wg_ref, wb_ref, wo_ref, o_ref, n_ref, *, tm, t_lat, base, nk, mixed):
    i = pl.program_id(1)
    k = pl.program_id(2)
    row0 = i * tm

    @pl.when(k == 0)
    def _():
        _prologue(h_ref, n_ref, o_ref, gpre_ref, modl_ref, modc_ref, base, row0, tm, t_lat, mixed)

    n = n_ref[...]
    y = None
    for bi, br_ref in enumerate((b0_ref, b1_ref, b2_ref, b3_ref)):
        t = jax.nn.sigmoid(_dot(n, wg_ref[bi])) * _dot(br_ref[0], wb_ref[bi])
        y = t if y is None else y + t
    _accumulate(o_ref, y.astype(BF16), wo_ref)

    @pl.when(k == nk - 1)
    def _():
        _epilogue(h_ref, o_ref, gpost_ref, modl_ref, modc_ref, base + 2, 1.0, row0, tm, t_lat, mixed)


def merge_sublayer(h, modl, modc, g_pre, g_post, branches, wg, wb, wo, *, base, t_lat, rows, tm, tn=256):
    bsz, _, d = h.shape
    w = branches[0].shape[-1]
    nk = d // tn
    mixed = rows > t_lat
    kern = functools.partial(_merge_kernel, tm=tm, t_lat=t_lat, base=base, nk=nk, mixed=mixed)
    br_spec = pl.BlockSpec((1, tm, w), lambda b, i, k: (b, i, 0))
    return pl.pallas_call(
        kern,
        out_shape=jax.ShapeDtypeStruct((bsz, rows, d), F32),
        grid=(bsz, rows // tm, nk),
        in_specs=[
            pl.BlockSpec((1, tm, d), lambda b, i, k: (b, i, 0)),
            pl.BlockSpec((1, N_MOD, d), lambda b, i, k: (b, 0, 0)),
            pl.BlockSpec((N_MOD, d), lambda b, i, k: (0, 0)),
            pl.BlockSpec((1, d), lambda b, i, k: (0, 0)),
            pl.BlockSpec((1, d), lambda b, i, k: (0, 0)),
            br_spec, br_spec, br_spec, br_spec,
            pl.BlockSpec((4, d, tn), lambda b, i, k: (0, 0, k)),
            pl.BlockSpec((4, w, tn), lambda b, i, k: (0, 0, k)),
            pl.BlockSpec((tn, d), lambda b, i, k: (k, 0)),
        ],
        out_specs=pl.BlockSpec((1, tm, d), lambda b, i, k: (b, i, 0)),
        scratch_shapes=[pltpu.VMEM((tm, d), BF16)],
        compiler_params=_cparams(("parallel", "parallel", "arbitrary")),
        name="merge_sublayer",
    )(h, modl, modc, g_pre.reshape(1, d), g_post.reshape(1, d), *branches, wg, wb, wo)


def _dwconv_rows(x, w, b, t_lat):
    s = x.shape[0]
    t = lax.broadcasted_iota(jnp.int32, (s, 1), 0)
    tl = jnp.where(t >= t_lat, t - t_lat, t)
    seg_len = jnp.where(t >= t_lat, s - t_lat, t_lat)
    y = x * w[1:2, :] + b
    for j, d in ((0, -1), (2, 1), (3, 2)):
        xs = pltpu.roll(x, (-d) % s, 0)
        ok = (tl + d >= 0) & (tl + d < seg_len)
        y = y + jnp.where(ok, xs, 0.0) * w[j:j + 1, :]
    return y


def _conv_silu_kernel(x_ref, w_ref, b_ref, o_ref, *, t_lat):
    y = _dwconv_rows(x_ref[0], w_ref[...], b_ref[...], t_lat)
    o_ref[0] = y * jax.nn.sigmoid(y)


def ssm_conv(p, conv_w, conv_b, *, col0, t_lat, tc=512):
    bsz, s, _ = p.shape
    c = conv_w.shape[1]
    off = col0 // tc
    return pl.pallas_call(
        functools.partial(_conv_silu_kernel, t_lat=t_lat),
        out_shape=jax.ShapeDtypeStruct((bsz, s, c), F32),
        grid=(bsz, c // tc),
        in_specs=[
            pl.BlockSpec((1, s, tc), lambda b, j: (b, 0, off + j)),
            pl.BlockSpec((CONV_K, tc), lambda b, j: (0, j)),
            pl.BlockSpec((1, tc), lambda b, j: (0, j)),
        ],
        out_specs=pl.BlockSpec((1, s, tc), lambda b, j: (b, 0, j)),
        compiler_params=_cparams(("parallel", "parallel")),
        name="ssm_conv",
    )(p, conv_w, conv_b.reshape(1, c))


def _ssd_direction(xc, dtraw, a, dtb, expand, state_ref, reverse):
    q = xc.shape[0]
    nh = dtraw.shape[1]
    xw = nh * SSM_HEAD_DIM
    gw = xw // SSM_GROUPS
    hg = nh // SSM_GROUPS
    x = xc[:, :xw]
    dt = jax.nn.softplus(dtraw + dtb)
    d = dt * a
    li = lax.broadcasted_iota(jnp.int32, (q, q), 0)
    si = lax.broadcasted_iota(jnp.int32, (q, q), 1)
    mask = (si >= li) if reverse else (si <= li)
    tri = jnp.where(mask, 1.0, 0.0).astype(BF16)
    tri_t = jnp.where((li >= si) if reverse else (li <= si), 1.0, 0.0).astype(BF16)
    hi = d.astype(BF16)
    mid = (d - hi.astype(F32)).astype(BF16)
    lo = (d - hi.astype(F32) - mid.astype(F32)).astype(BF16)
    cs = _dot(tri, hi) + _dot(tri, mid) + _dot(tri, lo)
    cs_t = _dot_tn(hi, tri_t) + _dot_tn(mid, tri_t) + _dot_tn(lo, tri_t)
    last = 0 if reverse else q - 1
    total = cs[last:last + 1, :]
    w_state = jnp.exp(total - cs) * dt
    e_off = jnp.exp(cs)
    ex = _split_dot(jnp.concatenate([w_state, dt, e_off], axis=0), expand)
    x_state = (x * ex[:q]).astype(BF16)
    x_dt = x * ex[q:2 * q]
    e_off_x = ex[2 * q:]
    lane = lax.broadcasted_iota(jnp.int32, (q, LANE), 1)
    ys = []
    for g in range(SSM_GROUPS):
        bm = xc[:, xw + g * SSM_STATE: xw + (g + 1) * SSM_STATE].astype(BF16)
        cm = xc[:, xw + (SSM_GROUPS + g) * SSM_STATE: xw + (SSM_GROUPS + g + 1) * SSM_STATE].astype(BF16)
        cb = _dot_nt(cm, bm)
        st = state_ref[g]
        y_off = _dot(cm, st.astype(BF16)) * e_off_x[:, g * gw:(g + 1) * gw]
        yd = []
        for hp in range(hg // 2):
            ms = []
            for hh in range(2):
                h = g * hg + hp * 2 + hh
                seg = cs[:, h:h + 1] - cs_t[h:h + 1, :]
                ms.append((cb * jnp.where(mask, jnp.exp(seg), 0.0)).astype(BF16))
            c0 = g * gw + hp * LANE
            xp = x_dt[:, c0:c0 + LANE]
            rhs = jnp.concatenate([jnp.where(lane < SSM_HEAD_DIM, xp, 0.0).astype(BF16),
                                   jnp.where(lane >= SSM_HEAD_DIM, xp, 0.0).astype(BF16)], axis=0)
            yd.append(_dot(jnp.concatenate(ms, axis=1), rhs))
        ys.append(jnp.concatenate(yd, axis=1) + y_off)
        decay = e_off_x[last:last + 1, g * gw:(g + 1) * gw]
        state_ref[g] = decay * st + _dot_tn(bm, x_state[:, g * gw:(g + 1) * gw])
    return jnp.concatenate(ys, axis=1)


def _ssd_kernel(xf_ref, xb_ref, dtf_ref, dtb_ref, a_ref, bias_ref, exp_ref, yf_ref, yb_ref,
                sf_ref, sb_ref, *, nh):
    @pl.when(pl.program_id(1) == 0)
    def _():
        sf_ref[...] = jnp.zeros_like(sf_ref)
        sb_ref[...] = jnp.zeros_like(sb_ref)

    ex = exp_ref[...]
    yf_ref[0] = _ssd_direction(xf_ref[0], dtf_ref[0][:, :nh], a_ref[0:1, :], bias_ref[0:1, :],
                               ex, sf_ref, False)
    yb_ref[0] = _ssd_direction(xb_ref[0], dtb_ref[0][:, nh:2 * nh], a_ref[1:2, :], bias_ref[1:2, :],
                               ex, sb_ref, True)


def ssd_scan(xc, dt_raw, a_log, dt_bias, *, t_lat):
    bsz, s, cw = xc.shape
    nh = a_log.shape[1]
    xw = nh * SSM_HEAD_DIM
    q = SSM_CHUNK
    nt = s // q
    nl = t_lat // q
    nc = nt - nl
    a = -jnp.exp(a_log.astype(F32))
    expand = (jnp.arange(xw)[None, :] // SSM_HEAD_DIM == jnp.arange(nh)[:, None]).astype(BF16)

    def fidx(j):
        return jnp.where(j < nc, nl + j, j - nc)

    def bidx(j):
        return nt - 1 - j

    ndt = dt_raw.shape[-1]
    return pl.pallas_call(
        functools.partial(_ssd_kernel, nh=nh),
        out_shape=(jax.ShapeDtypeStruct((bsz, s, xw), F32), jax.ShapeDtypeStruct((bsz, s, xw), F32)),
        grid=(bsz, nt),
        in_specs=[
            pl.BlockSpec((1, q, cw), lambda b, j: (b, fidx(j), 0)),
            pl.BlockSpec((1, q, cw), lambda b, j: (b, bidx(j), 0)),
            pl.BlockSpec((1, q, ndt), lambda b, j: (b, fidx(j), 0)),
            pl.BlockSpec((1, q, ndt), lambda b, j: (b, bidx(j), 0)),
            pl.BlockSpec((2, nh), lambda b, j: (0, 0)),
            pl.BlockSpec((2, nh), lambda b, j: (0, 0)),
            pl.BlockSpec((nh, xw), lambda b, j: (0, 0)),
        ],
        out_specs=(pl.BlockSpec((1, q, xw), lambda b, j: (b, fidx(j), 0)),
                   pl.BlockSpec((1, q, xw), lambda b, j: (b, bidx(j), 0))),
        scratch_shapes=[pltpu.VMEM((SSM_GROUPS, SSM_STATE, xw // SSM_GROUPS), F32),
                        pltpu.VMEM((SSM_GROUPS, SSM_STATE, xw // SSM_GROUPS), F32)],
        compiler_params=_cparams(("parallel", "arbitrary")),
        name="ssd_scan",
    )(xc, xc, dt_raw, dt_raw, a, dt_bias.astype(F32), expand)


def _ssm_finish_kernel(yf_ref, yb_ref, xs_ref, z_ref, dsk_ref, g_ref, o_ref):
    z = z_ref[0]
    y = (yf_ref[0] + yb_ref[0] + dsk_ref[...] * xs_ref[0]) * (z * jax.nn.sigmoid(z))
    gw = y.shape[1] // SSM_GROUPS
    outs = []
    for g in range(SSM_GROUPS):
        outs.append(_rms(y[:, g * gw:(g + 1) * gw], g_ref[:, g * gw:(g + 1) * gw]))
    o_ref[0] = jnp.concatenate(outs, axis=1).astype(o_ref.dtype)


def ssm_finish(y_f, y_b, xc, p, d_skip, g_norm, *, z_col0, tm):
    bsz, s, xw = y_f.shape
    dsk = jnp.repeat(d_skip.astype(F32), SSM_HEAD_DIM).reshape(1, xw)
    zoff = z_col0 // xw
    spec = pl.BlockSpec((1, tm, xw), lambda b, i: (b, i, 0))
    return pl.pallas_call(
        _ssm_finish_kernel,
        out_shape=jax.ShapeDtypeStruct((bsz, s, xw), BF16),
        grid=(bsz, s // tm),
        in_specs=[spec, spec, spec,
                  pl.BlockSpec((1, tm, xw), lambda b, i: (b, i, zoff)),
                  pl.BlockSpec((1, xw), lambda b, i: (0, 0)),
                  pl.BlockSpec((1, xw), lambda b, i: (0, 0))],
        out_specs=spec,
        compiler_params=_cparams(("parallel", "parallel")),
        name="ssm_finish",
    )(y_f, y_b, xc, p, dsk, g_norm.reshape(1, xw))


def _rope_tables(t_lat, head_dim, reps):
    rows = t_lat // GRID_W
    row = jnp.repeat(jnp.arange(rows, dtype=jnp.int32), GRID_W).astype(F32)
    col = (jnp.arange(t_lat, dtype=jnp.int32) % GRID_W).astype(F32)
    half = head_dim // 2
    inv = ROPE_THETA ** (-(jnp.arange(0, half, 2, dtype=F32) / half))
    ang_r = row[:, None] * inv[None, :]
    ang_c = col[:, None] * inv[None, :]
    cos = jnp.concatenate([jnp.cos(ang_r), jnp.cos(ang_r), jnp.cos(ang_c), jnp.cos(ang_c)], axis=1)
    sin = jnp.concatenate([-jnp.sin(ang_r), jnp.sin(ang_r), -jnp.sin(ang_c), jnp.sin(ang_c)], axis=1)
    return jnp.tile(cos, (1, reps)), jnp.tile(sin, (1, reps))


def _rope(v, cos, sin, quarter):
    w = v.shape[1]
    lane = lax.broadcasted_iota(jnp.int32, v.shape, 1)
    first = (lane % (2 * quarter)) < quarter
    swapped = jnp.where(first, pltpu.roll(v, w - quarter, 1), pltpu.roll(v, quarter, 1))
    return v * cos + swapped * sin


def _softmax_parts(s):
    m = jnp.max(s, axis=-1, keepdims=True)
    e = jnp.exp(s - m)
    return e, jnp.sum(e, axis=-1, keepdims=True)


def _gqa_kernel(q_ref, k_ref, v_ref, cos_ref, sin_ref, cosq_ref, sinq_ref, gq_ref, gk_ref, o_ref,
                kp_ref, vp_ref, *, tq, t_lat, nq_lat):
    qi = pl.program_id(2)
    d = GQA_HEAD_DIM
    quarter = d // 4

    @pl.when(qi == 0)
    def _():
        kn = _rms(k_ref[0], gk_ref[...])
        kp_ref[:t_lat, :] = _rope(kn[:t_lat], cos_ref[...], sin_ref[...], quarter).astype(BF16)
        kp_ref[t_lat:, :] = kn[t_lat:].astype(BF16)
        vp_ref[...] = v_ref[0].astype(BF16)

    scale = d ** -0.5

    def attend(rope_q, k0):
        qs = []
        for j in range(GQA_GROUP):
            qh = _rms(q_ref[0][:, j * d:(j + 1) * d], gq_ref[...])
            if rope_q:
                qh = _rope(qh, cosq_ref[...], sinq_ref[...], quarter)
            qs.append((qh * scale).astype(BF16))
        qst = jnp.concatenate(qs, axis=0)
        s = _dot_nt(qst, kp_ref[k0:, :])
        e, l = _softmax_parts(s)
        o = _dot(e.astype(BF16), vp_ref[k0:, :]) / l
        for j in range(GQA_GROUP):
            o_ref[0, :, j * d:(j + 1) * d] = o[j * tq:(j + 1) * tq].astype(o_ref.dtype)

    @pl.when(qi < nq_lat)
    def _():
        attend(True, 0)

    @pl.when(qi >= nq_lat)
    def _():
        attend(False, t_lat)


def gqa_attention(p, g_q, g_k, *, q_col0, k_col0, v_col0, n_kv, t_lat, with_ctx, tq=128):
    bsz, s, _ = p.shape
    d = GQA_HEAD_DIM
    gw = GQA_GROUP * d
    nq_lat = t_lat // tq
    nq = (s if with_ctx else t_lat) // tq
    cos, sin = _rope_tables(t_lat, d, 1)
    qo, ko, vo = q_col0 // gw, k_col0 // d, v_col0 // d
    kern = functools.partial(_gqa_kernel, tq=tq, t_lat=t_lat, nq_lat=nq_lat)
    qtab = lambda b, h, i: (jnp.minimum(i, nq_lat - 1), 0)
    return pl.pallas_call(
        kern,
        out_shape=jax.ShapeDtypeStruct((bsz, nq * tq, n_kv * gw), BF16),
        grid=(bsz, n_kv, nq),
        in_specs=[
            pl.BlockSpec((1, tq, gw), lambda b, h, i: (b, i, qo + h)),
            pl.BlockSpec((1, s, d), lambda b, h, i: (b, 0, ko + h)),
            pl.BlockSpec((1, s, d), lambda b, h, i: (b, 0, vo + h)),
            pl.BlockSpec((t_lat, d), lambda b, h, i: (0, 0)),
            pl.BlockSpec((t_lat, d), lambda b, h, i: (0, 0)),
            pl.BlockSpec((tq, d), qtab),
            pl.BlockSpec((tq, d), qtab),
            pl.BlockSpec((1, d), lambda b, h, i: (0, 0)),
            pl.BlockSpec((1, d), lambda b, h, i: (0, 0)),
        ],
        out_specs=pl.BlockSpec((1, tq, gw), lambda b, h, i: (b, i, h)),
        scratch_shapes=[pltpu.VMEM((s, d), BF16), pltpu.VMEM((s, d), BF16)],
        compiler_params=_cparams(("parallel", "parallel", "arbitrary")),
        name="gqa_attention",
    )(p, p, p, cos, sin, cos, sin, g_q.reshape(1, d), g_k.reshape(1, d))


def _diff_kernel(q_ref, k_ref, v_ref, cos_ref, sin_ref, cosq_ref, sinq_ref, lq_ref, lk_ref, g_ref,
                 o_ref, kp_ref, vp_ref, *, t_lat, nq_lat, lambda_init):
    qi = pl.program_id(2)
    d = DIFF_HEAD_DIM
    quarter = d // 4

    @pl.when(qi == 0)
    def _():
        k = k_ref[0]
        kp_ref[:t_lat, :] = _rope(k[:t_lat], cos_ref[...], sin_ref[...], quarter).astype(BF16)
        kp_ref[t_lat:, :] = k[t_lat:].astype(BF16)
        vp_ref[...] = v_ref[0].astype(BF16)

    prod = lq_ref[...] * lk_ref[...]
    sums = jnp.sum(prod, axis=-1, keepdims=True)
    lam = jnp.exp(sums[0:1]) - jnp.exp(sums[1:2]) + lambda_init
    scale = d ** -0.5

    def attend(rope_q, k0):
        q = q_ref[0]
        if rope_q:
            q = _rope(q, cosq_ref[...], sinq_ref[...], quarter)
        q = q * scale
        lane = lax.broadcasted_iota(jnp.int32, q.shape, 1)
        kk = kp_ref[k0:, :]
        q0 = jnp.where(lane < d, q, 0.0).astype(BF16)
        q1 = jnp.where(lane >= d, q, 0.0).astype(BF16)
        e0, l0 = _softmax_parts(_dot_nt(q0, kk))
        e1, l1 = _softmax_parts(_dot_nt(q1, kk))
        pd = e0 / l0 - lam * (e1 / l1)
        o = _dot(pd.astype(BF16), vp_ref[k0:, :])
        o_ref[0] = (_rms(o, g_ref[...]) * (1.0 - lambda_init)).astype(o_ref.dtype)

    @pl.when(qi < nq_lat)
    def _():
        attend(True, 0)

    @pl.when(qi >= nq_lat)
    def _():
        attend(False, t_lat)


def diff_attention(p, lambda_q, lambda_k, g_norm, *, q_col0, k_col0, v_col0, n_heads, t_lat,
                   with_ctx, lambda_init, tq=256):
    bsz, s, _ = p.shape
    hw = 2 * DIFF_HEAD_DIM
    nq_lat = t_lat // tq
    nq = (s if with_ctx else t_lat) // tq
    cos, sin = _rope_tables(t_lat, DIFF_HEAD_DIM, 2)
    qo, ko, vo = q_col0 // hw, k_col0 // hw, v_col0 // hw
    kern = functools.partial(_diff_kernel, t_lat=t_lat, nq_lat=nq_lat, lambda_init=lambda_init)
    qtab = lambda b, h, i: (jnp.minimum(i, nq_lat - 1), 0)
    return pl.pallas_call(
        kern,
        out_shape=jax.ShapeDtypeStruct((bsz, nq * tq, n_heads * hw), BF16),
        grid=(bsz, n_heads, nq),
        in_specs=[
            pl.BlockSpec((1, tq, hw), lambda b, h, i: (b, i, qo + h)),
            pl.BlockSpec((1, s, hw), lambda b, h, i: (b, 0, ko + h)),
            pl.BlockSpec((1, s, hw), lambda b, h, i: (b, 0, vo + h)),
            pl.BlockSpec((t_lat, hw), lambda b, h, i: (0, 0)),
            pl.BlockSpec((t_lat, hw), lambda b, h, i: (0, 0)),
            pl.BlockSpec((tq, hw), qtab),
            pl.BlockSpec((tq, hw), qtab),
            pl.BlockSpec((2, DIFF_HEAD_DIM), lambda b, h, i: (0, 0)),
            pl.BlockSpec((2, DIFF_HEAD_DIM), lambda b, h, i: (0, 0)),
            pl.BlockSpec((1, hw), lambda b, h, i: (0, 0)),
        ],
        out_specs=pl.BlockSpec((1, tq, hw), lambda b, h, i: (b, i, h)),
        scratch_shapes=[pltpu.VMEM((s, hw), BF16), pltpu.VMEM((s, hw), BF16)],
        compiler_params=_cparams(("parallel", "parallel", "arbitrary")),
        name="diff_attention",
    )(p, p, p, cos, sin, cos, sin, lambda_q.astype(F32), lambda_k.astype(F32), g_norm.reshape(1, hw))


def _scan_rows(a_ref, b_ref, h_ref, row0, nblk, carry, reverse, accumulate):
    rows = lax.broadcasted_iota(jnp.int32, (SUBLANE, LANE), 0)

    def body(i, carry):
        blk = (nblk - 1 - i) if reverse else i
        r0 = pl.multiple_of(row0 + blk * SUBLANE, SUBLANE)
        a = a_ref[pl.ds(r0, SUBLANE), :]
        b = b_ref[pl.ds(r0, SUBLANE), :]
        for sft in (1, 2, 4):
            if reverse:
                a_s = pltpu.roll(a, SUBLANE - sft, 0)
                b_s = pltpu.roll(b, SUBLANE - sft, 0)
                ok = rows < SUBLANE - sft
            else:
                a_s = pltpu.roll(a, sft, 0)
                b_s = pltpu.roll(b, sft, 0)
                ok = rows >= sft
            b = jnp.where(ok, a * b_s + b, b)
            a = jnp.where(ok, a * a_s, a)
        h = b + a * carry
        if accumulate:
            h_ref[pl.ds(r0, SUBLANE), :] += h
        else:
            h_ref[pl.ds(r0, SUBLANE), :] = h
        return h[0:1, :] if reverse else h[SUBLANE - 1:SUBLANE, :]

    return lax.fori_loop(0, nblk, body, carry, unroll=4)


def _lru_kernel(x_ref, g_ref, cw_ref, cb_ref, wrg_ref, wig_ref, brg_ref, big_ref, lam_ref, o_ref,
                a_ref, b_ref, h_ref, *, t_lat):
    s = x_ref.shape[1]
    xr = _dwconv_rows(x_ref[0], cw_ref[...], cb_ref[...], t_lat)
    xb = xr.astype(BF16)
    n_lat = t_lat // SUBLANE
    n_ctx = (s - t_lat) // SUBLANE
    zero = jnp.zeros((1, LANE), F32)
    for dr in range(2):
        r = jax.nn.sigmoid(_dot(xb, wrg_ref[dr, 0]) + brg_ref[dr:dr + 1, :])
        ig = jax.nn.sigmoid(_dot(xb, wig_ref[dr, 0]) + big_ref[dr:dr + 1, :])
        log_a = -LRU_C * r * jax.nn.softplus(-lam_ref[dr:dr + 1, :])
        a = jnp.exp(log_a)
        a_ref[...] = a
        b_ref[...] = jnp.sqrt(-jnp.tanh(log_a) * (a * a + 1.0)) * (ig * xr)
        if dr == 0:
            c = _scan_rows(a_ref, b_ref, h_ref, t_lat, n_ctx, zero, False, False)
            _scan_rows(a_ref, b_ref, h_ref, 0, n_lat, c, False, False)
        else:
            c = _scan_rows(a_ref, b_ref, h_ref, t_lat, n_ctx, zero, True, True)
            _scan_rows(a_ref, b_ref, h_ref, 0, n_lat, c, True, True)
    o_ref[0] = (jax.nn.gelu(g_ref[0]) * h_ref[...]).astype(o_ref.dtype)


def rglru(p, conv_w, conv_b, w_rg, b_rg, w_ig, b_ig, lam, *, g_col0, x_col0, t_lat):
    bsz, s, _ = p.shape
    width = conv_w.shape[1]
    ncg = width // LANE
    go, xo = g_col0 // LANE, x_col0 // LANE

    def pair_blocks(w):
        nb = w.shape[1]
        w = w.reshape(2, nb // 2, 2, LRU_BLOCK_DIM, LRU_BLOCK_DIM)
        z = jnp.zeros_like(w[:, :, 0])
        top = jnp.concatenate([w[:, :, 0], z], axis=-1)
        bot = jnp.concatenate([z, w[:, :, 1]], axis=-1)
        return jnp.concatenate([top, bot], axis=-2).astype(BF16)

    wspec = pl.BlockSpec((2, 1, LANE, LANE), lambda b, c: (0, c, 0, 0))
    vspec = pl.BlockSpec((2, LANE), lambda b, c: (0, c))
    return pl.pallas_call(
        functools.partial(_lru_kernel, t_lat=t_lat),
        out_shape=jax.ShapeDtypeStruct((bsz, s, width), BF16),
        grid=(bsz, ncg),
        in_specs=[
            pl.BlockSpec((1, s, LANE), lambda b, c: (b, 0, xo + c)),
            pl.BlockSpec((1, s, LANE), lambda b, c: (b, 0, go + c)),
            pl.BlockSpec((CONV_K, LANE), lambda b, c: (0, c)),
            pl.BlockSpec((1, LANE), lambda b, c: (0, c)),
            wspec, wspec, vspec, vspec, vspec,
        ],
        out_specs=pl.BlockSpec((1, s, LANE), lambda b, c: (b, 0, c)),
        scratch_shapes=[pltpu.VMEM((s, LANE), F32), pltpu.VMEM((s, LANE), F32), pltpu.VMEM((s, LANE), F32)],
        compiler_params=_cparams(("parallel", "parallel")),
        name="rglru",
    )(p, p, conv_w.astype(F32), conv_b.reshape(1, width).astype(F32), pair_blocks(w_rg), pair_blocks(w_ig),
      b_rg.astype(F32), b_ig.astype(F32), lam.astype(F32))


def _split_w_in(w_in, mix_w, n_ssm_heads):
    gn = SSM_GROUPS * SSM_STATE
    kv = mix_w // GQA_GROUP
    sizes = (mix_w, mix_w + 2 * gn, 2 * n_ssm_heads, mix_w, mix_w, mix_w, mix_w, kv, kv, mix_w, mix_w)
    names = ("z", "xbc", "dt", "q_d", "k_d", "v_d", "q_g", "k_g", "v_g", "g_lru", "x_lru")
    parts = {}
    o = 0
    for nme, sz in zip(names, sizes):
        parts[nme] = w_in[:, o:o + sz]
        o += sz
    order = ("q_g", "q_d", "k_d", "v_d", "z", "g_lru", "x_lru", "xbc", "k_g", "v_g")
    offs = {}
    o = 0
    for nme in order:
        offs[nme] = o
        o += parts[nme].shape[1]
    w_main = jnp.concatenate([parts[nme] for nme in order], axis=1).astype(BF16)
    w_dt = jnp.pad(parts["dt"], ((0, 0), (0, LANE - 2 * n_ssm_heads))).astype(BF16)
    return w_main, w_dt, offs


def kernel(x, c, ctx, c_ctx, w_ada, b_ada, g_pre, g_post, w_ffn_gate, w_ffn_up, w_ffn_down, w_in, conv_w_ssm, conv_b_ssm, a_log, dt_bias, d_skip, g_ssm_norm, lambda_q, lambda_k, g_diff_norm, g_q_norm, g_k_norm, conv_w_lru, conv_b_lru, w_rg, b_rg, w_ig, b_ig, lru_lambda, w_branch, w_gate, w_out):
    bsz, t_lat, d = x.shape
    depth = w_ada.shape[0]
    mix_w = d // 2
    n_ssm_heads = a_log.shape[-1]
    s = t_lat + ctx.shape[1]
    tm_all = s // 4 if (s // 4) % 16 == 0 else s // 3
    tm_lat = 512

    r = ((bsz + 1 + 15) // 16) * 16
    cc = jnp.zeros((r, d), F32).at[:bsz].set(c).at[bsz].set(c_ctx)
    mods = adaln(cc, w_ada, b_ada)

    h = jnp.concatenate([x, ctx], axis=1)
    for l in range(depth):
        last = l == depth - 1
        lambda_init = 0.8 - 0.6 * math.exp(-0.3 * l)
        modl = mods[l, :bsz].reshape(bsz, N_MOD, d)
        modc = mods[l, bsz].reshape(N_MOD, d)
        w_main, w_dt, offs = _split_w_in(w_in[l], mix_w, n_ssm_heads)

        h = ffn_sublayer(h, modl, modc, g_pre[l, 0], g_post[l, 0],
                         w_ffn_gate[l, 0].astype(BF16), w_ffn_up[l, 0].astype(BF16),
                         w_ffn_down[l, 0].astype(BF16), base=0, t_lat=t_lat, rows=s, tm=tm_all)

        p, dt_raw = in_projection(h, modl, modc, g_pre[l, 1], w_main, w_dt, base=3, t_lat=t_lat, tm=tm_all)

        xc = ssm_conv(p, conv_w_ssm[l].astype(F32), conv_b_ssm[l].astype(F32), col0=offs["xbc"], t_lat=t_lat)
        y_f, y_b = ssd_scan(xc, dt_raw, a_log[l], dt_bias[l], t_lat=t_lat)
        ssm = ssm_finish(y_f, y_b, xc, p, d_skip[l], g_ssm_norm[l], z_col0=offs["z"], tm=tm_all)

        dif = diff_attention(p, lambda_q[l], lambda_k[l], g_diff_norm[l], q_col0=offs["q_d"],
                             k_col0=offs["k_d"], v_col0=offs["v_d"], n_heads=mix_w // (2 * DIFF_HEAD_DIM),
                             t_lat=t_lat, with_ctx=not last, lambda_init=lambda_init)
        gqa = gqa_attention(p, g_q_norm[l], g_k_norm[l], q_col0=offs["q_g"], k_col0=offs["k_g"],
                            v_col0=offs["v_g"], n_kv=mix_w // (GQA_GROUP * GQA_HEAD_DIM), t_lat=t_lat,
                            with_ctx=not last)
        lru = rglru(p, conv_w_lru[l], conv_b_lru[l], w_rg[l], b_rg[l], w_ig[l], b_ig[l], lru_lambda[l],
                    g_col0=offs["g_lru"], x_col0=offs["x_lru"], t_lat=t_lat)

        rows = t_lat if last else s
        tm = tm_lat if last else tm_all
        h = merge_sublayer(h, modl, modc, g_pre[l, 1], g_post[l, 1], (ssm, dif, gqa, lru),
                           w_gate[l].astype(BF16), w_branch[l].astype(BF16), w_out[l].astype(BF16),
                           base=3, t_lat=t_lat, rows=rows, tm=tm)
        h = ffn_sublayer(h, modl, modc, g_pre[l, 2], g_post[l, 2],
                         w_ffn_gate[l, 1].astype(BF16), w_ffn_up[l, 1].astype(BF16),
                         w_ffn_down[l, 1].astype(BF16), base=6, t_lat=t_lat, rows=rows, tm=tm)
    return h
```

```python
import functools
import math

import jax
import jax.numpy as jnp
from jax import lax
from jax.experimental import pallas as pl
from jax.experimental.pallas import tpu as pltpu

F32 = jnp.float32
BF16 = jnp.bfloat16

GRID_W = 64
N_MOD = 9
FFN_HALF = 0.5
EPS = 1e-6
ROPE_THETA = 10000.0
CONV_K = 4
SSM_HEAD_DIM = 64
SSM_GROUPS = 2
SSM_STATE = 128
SSM_CHUNK = 128
DIFF_HEAD_DIM = 64
GQA_HEAD_DIM = 128
GQA_GROUP = 4
LRU_BLOCK_DIM = 64
LRU_C = 8.0

LANE = 128
SUBLANE = 8
V7X_VMEM_BYTES = 64 * 1024 * 1024
VMEM_LIMIT = V7X_VMEM_BYTES - 6 * 1024 * 1024


def _cparams(sem):
    return pltpu.CompilerParams(dimension_semantics=sem, vmem_limit_bytes=VMEM_LIMIT)


def _dot(a, b):
    return jnp.dot(a, b, preferred_element_type=F32)


def _dot_nt(a, b):
    return lax.dot_general(a, b, (((1,), (1,)), ((), ())), preferred_element_type=F32)


def _dot_tn(a, b):
    return lax.dot_general(a, b, (((0,), (0,)), ((), ())), preferred_element_type=F32)


def _rms(x, g):
    return x * lax.rsqrt(jnp.mean(x * x, axis=-1, keepdims=True) + EPS) * g


def _split_dot(a_f32, b_bf16):
    hi = a_f32.astype(BF16)
    lo = (a_f32 - hi.astype(F32)).astype(BF16)
    return _dot(hi, b_bf16) + _dot(lo, b_bf16)


def _adaln_kernel(c_ref, w_ref, b_ref, o_ref):
    x = c_ref[...]
    x = (x * jax.nn.sigmoid(x)).astype(BF16)
    o_ref[0] = _dot(x, w_ref[0].astype(BF16)) + b_ref[0]


def adaln(cc, w_ada, b_ada, *, tn=1024):
    depth, d, n = w_ada.shape
    r = cc.shape[0]
    return pl.pallas_call(
        _adaln_kernel,
        out_shape=jax.ShapeDtypeStruct((depth, r, n), F32),
        grid=(depth, n // tn),
        in_specs=[
            pl.BlockSpec((r, d), lambda l, j: (0, 0)),
            pl.BlockSpec((1, d, tn), lambda l, j: (l, 0, j)),
            pl.BlockSpec((1, 1, tn), lambda l, j: (l, 0, j)),
        ],
        out_specs=pl.BlockSpec((1, r, tn), lambda l, j: (l, 0, j)),
        compiler_params=_cparams(("parallel", "parallel")),
        name="adaln",
    )(cc, w_ada, b_ada.reshape(depth, 1, n))


ROW_CHUNK = 16
COL_CHUNK = 512


def _for_chunks(lo, hi, body, unroll):
    if hi <= lo:
        return

    def step(r, carry):
        body(pl.ds(pl.multiple_of(r * ROW_CHUNK, ROW_CHUNK), ROW_CHUNK))
        return carry

    lax.fori_loop(lo, hi, step, 0, unroll=min(unroll, hi - lo))


def _segments(i, tm, rows, t_lat, fn):
    n = tm // ROW_CHUNK
    if rows <= t_lat:
        fn(0, n, False)
        return
    last = rows // tm - 1
    assert last * tm <= t_lat, (tm, rows, t_lat)
    nb = (t_lat - last * tm) // ROW_CHUNK

    @pl.when(i < last)
    def _():
        fn(0, n, False)

    @pl.when(i == last)
    def _():
        fn(0, nb, False)
        fn(nb, n, True)


def _row_stats(read, inv_ref, tm):
    def stats(sl):
        x = read(sl)
        inv = lax.rsqrt(jnp.mean(x * x, axis=-1, keepdims=True) + EPS)
        inv_ref[sl, :] = jnp.broadcast_to(inv, (ROW_CHUNK, LANE))

    _for_chunks(0, tm // ROW_CHUNK, stats, 8)


def _lanes(inv_tile, width):
    return jnp.concatenate([inv_tile] * (width // LANE), axis=1)


def _prologue(i, h_ref, n_ref, o_ref, inv_ref, gpre_ref, modl_ref, modc_ref, base, tm, rows, t_lat):
    d = h_ref.shape[-1]
    _row_stats(lambda sl: h_ref[0, sl, :], inv_ref, tm)
    g = gpre_ref[...]

    def seg(lo, hi, is_ctx):
        mod = (lambda j: modc_ref[j:j + 1, :]) if is_ctx else (lambda j: modl_ref[0, j:j + 1, :])
        shift = mod(base)
        gain = g * (1.0 + mod(base + 1))

        def apply(sl):
            n_ref[sl, :] = (h_ref[0, sl, :] * _lanes(inv_ref[sl, :], d) * gain + shift).astype(BF16)
            if o_ref is not None:
                o_ref[0, sl, :] = jnp.zeros((ROW_CHUNK, d), F32)

        _for_chunks(lo, hi, apply, 4)

    _segments(i, tm, rows, t_lat, seg)


def _accumulate(o_ref, a, w_ref):
    for c0 in range(0, o_ref.shape[-1], COL_CHUNK):
        o_ref[0, :, c0:c0 + COL_CHUNK] += _dot(a, w_ref[:, c0:c0 + COL_CHUNK])


def _epilogue(i, h_ref, o_ref, inv_ref, gpost_ref, modl_ref, modc_ref, gate_idx, factor, tm, rows, t_lat):
    d = h_ref.shape[-1]
    _row_stats(lambda sl: o_ref[0, sl, :], inv_ref, tm)
    g = gpost_ref[...] * factor

    def seg(lo, hi, is_ctx):
        gain = g * (modc_ref[gate_idx:gate_idx + 1, :] if is_ctx else modl_ref[0, gate_idx:gate_idx + 1, :])

        def apply(sl):
            o_ref[0, sl, :] = h_ref[0, sl, :] + o_ref[0, sl, :] * _lanes(inv_ref[sl, :], d) * gain

        _for_chunks(lo, hi, apply, 4)

    _segments(i, tm, rows, t_lat, seg)


def _ffn_kernel(h_ref, modl_ref, modc_ref, gpre_ref, gpost_ref, wg_ref, wu_ref, wd_ref,
                o_ref, n_ref, inv_ref, *, tm, rows, t_lat, base, nk):
    i = pl.program_id(1)
    k = pl.program_id(2)

    @pl.when(k == 0)
    def _():
        _prologue(i, h_ref, n_ref, o_ref, inv_ref, gpre_ref, modl_ref, modc_ref, base, tm, rows, t_lat)

    n = n_ref[...]
    g = _dot(n, wg_ref[...])
    u = _dot(n, wu_ref[...])
    a = (g * jax.nn.sigmoid(g) * u).astype(BF16)
    _accumulate(o_ref, a, wd_ref)

    @pl.when(k == nk - 1)
    def _():
        _epilogue(i, h_ref, o_ref, inv_ref, gpost_ref, modl_ref, modc_ref, base + 2, FFN_HALF, tm, rows, t_lat)


def ffn_sublayer(h, modl, modc, g_pre, g_post, wg, wu, wd, *, base, t_lat, rows, tm, tf=512):
    bsz, _, d = h.shape
    dff = wg.shape[1]
    nk = dff // tf
    kern = functools.partial(_ffn_kernel, tm=tm, rows=rows, t_lat=t_lat, base=base, nk=nk)
    return pl.pallas_call(
        kern,
        out_shape=jax.ShapeDtypeStruct((bsz, rows, d), F32),
        grid=(bsz, rows // tm, nk),
        in_specs=[
            pl.BlockSpec((1, tm, d), lambda b, i, k: (b, i, 0)),
            pl.BlockSpec((1, N_MOD, d), lambda b, i, k: (b, 0, 0)),
            pl.BlockSpec((N_MOD, d), lambda b, i, k: (0, 0)),
            pl.BlockSpec((1, d), lambda b, i, k: (0, 0)),
            pl.BlockSpec((1, d), lambda b, i, k: (0, 0)),
            pl.BlockSpec((d, tf), lambda b, i, k: (0, k)),
            pl.BlockSpec((d, tf), lambda b, i, k: (0, k)),
            pl.BlockSpec((tf, d), lambda b, i, k: (k, 0)),
        ],
        out_specs=pl.BlockSpec((1, tm, d), lambda b, i, k: (b, i, 0)),
        scratch_shapes=[pltpu.VMEM((tm, d), BF16), pltpu.VMEM((tm, LANE), F32)],
        compiler_params=_cparams(("parallel", "parallel", "arbitrary")),
        name="ffn_sublayer",
    )(h, modl, modc, g_pre.reshape(1, d), g_post.reshape(1, d), wg, wu, wd)


def _proj_kernel(h_ref, modl_ref, modc_ref, gpre_ref, w_ref, wdt_ref, o_ref, odt_ref, n_ref, inv_ref,
                 *, tm, rows, t_lat, base):
    i = pl.program_id(1)
    j = pl.program_id(2)

    @pl.when(j == 0)
    def _():
        _prologue(i, h_ref, n_ref, None, inv_ref, gpre_ref, modl_ref, modc_ref, base, tm, rows, t_lat)
        odt_ref[0] = _dot(n_ref[...], wdt_ref[...])

    o_ref[0] = _dot(n_ref[...], w_ref[...])


def in_projection(h, modl, modc, g_pre, w_main, w_dt, *, base, t_lat, tm, tn=512):
    bsz, s, d = h.shape
    ncol = w_main.shape[1]
    ndt = w_dt.shape[1]
    kern = functools.partial(_proj_kernel, tm=tm, rows=s, t_lat=t_lat, base=base)
    return pl.pallas_call(
        kern,
        out_shape=(jax.ShapeDtypeStruct((bsz, s, ncol), F32),
                   jax.ShapeDtypeStruct((bsz, s, ndt), F32)),
        grid=(bsz, s // tm, ncol // tn),
        in_specs=[
            pl.BlockSpec((1, tm, d), lambda b, i, j: (b, i, 0)),
            pl.BlockSpec((1, N_MOD, d), lambda b, i, j: (b, 0, 0)),
            pl.BlockSpec((N_MOD, d), lambda b, i, j: (0, 0)),
            pl.BlockSpec((1, d), lambda b, i, j: (0, 0)),
            pl.BlockSpec((d, tn), lambda b, i, j: (0, j)),
            pl.BlockSpec((d, ndt), lambda b, i, j: (0, 0)),
        ],
        out_specs=(pl.BlockSpec((1, tm, tn), lambda b, i, j: (b, i, j)),
                   pl.BlockSpec((1, tm, ndt), lambda b, i, j: (b, i, 0))),
        scratch_shapes=[pltpu.VMEM((tm, d), BF16), pltpu.VMEM((tm, LANE), F32)],
        compiler_params=_cparams(("parallel", "parallel", "arbitrary")),
        name="in_projection",
    )(h, modl, modc, g_pre.reshape(1, d), w_main, w_dt)


def _merge_kernel(h_ref, modl_ref, modc_ref, gpre_ref, gpost_ref, b0_ref, b1_ref, b2_ref, b3_ref,
                  wg_ref, wb_ref, wo_ref, o_ref, n_ref, inv_ref, *, tm, rows, t_lat, base, nk):
    i = pl.program_id(1)
    k = pl.program_id(2)

    @pl.when(k == 0)
    def _():
        _prologue(i, h_ref, n_ref, o_ref, inv_ref, gpre_ref, modl_ref, modc_ref, base, tm, rows, t_lat)

    n = n_ref[...]
    y = None
    for bi, br_ref in enumerate((b0_ref, b1_ref, b2_ref, b3_ref)):
        t = jax.nn.sigmoid(_dot(n, wg_ref[bi])) * _dot(br_ref[0], wb_ref[bi])
        y = t if y is None else y + t
    _accumulate(o_ref, y.astype(BF16), wo_ref)

    @pl.when(k == nk - 1)
    def _():
        _epilogue(i, h_ref, o_ref, inv_ref, gpost_ref, modl_ref, modc_ref, base + 2, 1.0, tm, rows, t_lat)


def merge_sublayer(h, modl, modc, g_pre, g_post, branches, wg, wb, wo, *, base, t_lat, rows, tm, tn=256):
    bsz, _, d = h.shape
    w = branches[0].shape[-1]
    nk = d // tn
    kern = functools.partial(_merge_kernel, tm=tm, rows=rows, t_lat=t_lat, base=base, nk=nk)
    br_spec = pl.BlockSpec((1, tm, w), lambda b, i, k: (b, i, 0))
    return pl.pallas_call(
        kern,
        out_shape=jax.ShapeDtypeStruct((bsz, rows, d), F32),
        grid=(bsz, rows // tm, nk),
        in_specs=[
            pl.BlockSpec((1, tm, d), lambda b, i, k: (b, i, 0)),
            pl.BlockSpec((1, N_MOD, d), lambda b, i, k: (b, 0, 0)),
            pl.BlockSpec((N_MOD, d), lambda b, i, k: (0, 0)),
            pl.BlockSpec((1, d), lambda b, i, k: (0, 0)),
            pl.BlockSpec((1, d), lambda b, i, k: (0, 0)),
            br_spec, br_spec, br_spec, br_spec,
            pl.BlockSpec((4, d, tn), lambda b, i, k: (0, 0, k)),
            pl.BlockSpec((4, w, tn), lambda b, i, k: (0, 0, k)),
            pl.BlockSpec((tn, d), lambda b, i, k: (k, 0)),
        ],
        out_specs=pl.BlockSpec((1, tm, d), lambda b, i, k: (b, i, 0)),
        scratch_shapes=[pltpu.VMEM((tm, d), BF16), pltpu.VMEM((tm, LANE), F32)],
        compiler_params=_cparams(("parallel", "parallel", "arbitrary")),
        name="merge_sublayer",
    )(h, modl, modc, g_pre.reshape(1, d), g_post.reshape(1, d), *branches, wg, wb, wo)


def _dwconv_rows(x, w, b, t_lat):
    s = x.shape[0]
    t = lax.broadcasted_iota(jnp.int32, (s, 1), 0)
    tl = jnp.where(t >= t_lat, t - t_lat, t)
    seg_len = jnp.where(t >= t_lat, s - t_lat, t_lat)
    y = x * w[1:2, :] + b
    for j, d in ((0, -1), (2, 1), (3, 2)):
        xs = pltpu.roll(x, (-d) % s, 0)
        ok = (tl + d >= 0) & (tl + d < seg_len)
        y = y + jnp.where(ok, xs, 0.0) * w[j:j + 1, :]
    return y


def _conv_silu_kernel(x_ref, w_ref, b_ref, o_ref, *, t_lat):
    y = _dwconv_rows(x_ref[0], w_ref[...], b_ref[...], t_lat)
    o_ref[0] = y * jax.nn.sigmoid(y)


def ssm_conv(p, conv_w, conv_b, *, col0, t_lat, tc=512):
    bsz, s, _ = p.shape
    c = conv_w.shape[1]
    off = col0 // tc
    return pl.pallas_call(
        functools.partial(_conv_silu_kernel, t_lat=t_lat),
        out_shape=jax.ShapeDtypeStruct((bsz, s, c), F32),
        grid=(bsz, c // tc),
        in_specs=[
            pl.BlockSpec((1, s, tc), lambda b, j: (b, 0, off + j)),
            pl.BlockSpec((CONV_K, tc), lambda b, j: (0, j)),
            pl.BlockSpec((1, tc), lambda b, j: (0, j)),
        ],
        out_specs=pl.BlockSpec((1, s, tc), lambda b, j: (b, 0, j)),
        compiler_params=_cparams(("parallel", "parallel")),
        name="ssm_conv",
    )(p, conv_w, conv_b.reshape(1, c))


def _ssd_direction(xc, dtraw, a, dtb, expand, state_ref, reverse):
    q = xc.shape[0]
    nh = dtraw.shape[1]
    xw = nh * SSM_HEAD_DIM
    gw = xw // SSM_GROUPS
    hg = nh // SSM_GROUPS
    x = xc[:, :xw]
    dt = jax.nn.softplus(dtraw + dtb)
    d = dt * a
    li = lax.broadcasted_iota(jnp.int32, (q, q), 0)
    si = lax.broadcasted_iota(jnp.int32, (q, q), 1)
    mask = (si >= li) if reverse else (si <= li)
    tri = jnp.where(mask, 1.0, 0.0).astype(BF16)
    tri_t = jnp.where((li >= si) if reverse else (li <= si), 1.0, 0.0).astype(BF16)
    hi = d.astype(BF16)
    mid = (d - hi.astype(F32)).astype(BF16)
    lo = (d - hi.astype(F32) - mid.astype(F32)).astype(BF16)
    cs = _dot(tri, hi) + _dot(tri, mid) + _dot(tri, lo)
    cs_t = _dot_tn(hi, tri_t) + _dot_tn(mid, tri_t) + _dot_tn(lo, tri_t)
    last = 0 if reverse else q - 1
    total = cs[last:last + 1, :]
    w_state = jnp.exp(total - cs) * dt
    e_off = jnp.exp(cs)
    ex = _split_dot(jnp.concatenate([w_state, dt, e_off], axis=0), expand)
    x_state = (x * ex[:q]).astype(BF16)
    x_dt = x * ex[q:2 * q]
    e_off_x = ex[2 * q:]
    lane = lax.broadcasted_iota(jnp.int32, (q, LANE), 1)
    ys = []
    for g in range(SSM_GROUPS):
        bm = xc[:, xw + g * SSM_STATE: xw + (g + 1) * SSM_STATE].astype(BF16)
        cm = xc[:, xw + (SSM_GROUPS + g) * SSM_STATE: xw + (SSM_GROUPS + g + 1) * SSM_STATE].astype(BF16)
        cb = _dot_nt(cm, bm)
        st = state_ref[g]
        y_off = _dot(cm, st.astype(BF16)) * e_off_x[:, g * gw:(g + 1) * gw]
        yd = []
        for hp in range(hg // 2):
            ms = []
            for hh in range(2):
                h = g * hg + hp * 2 + hh
                seg = cs[:, h:h + 1] - cs_t[h:h + 1, :]
                ms.append((cb * jnp.where(mask, jnp.exp(seg), 0.0)).astype(BF16))
            c0 = g * gw + hp * LANE
            xp = x_dt[:, c0:c0 + LANE]
            rhs = jnp.concatenate([jnp.where(lane < SSM_HEAD_DIM, xp, 0.0).astype(BF16),
                                   jnp.where(lane >= SSM_HEAD_DIM, xp, 0.0).astype(BF16)], axis=0)
            yd.append(_dot(jnp.concatenate(ms, axis=1), rhs))
        ys.append(jnp.concatenate(yd, axis=1) + y_off)
        decay = e_off_x[last:last + 1, g * gw:(g + 1) * gw]
        state_ref[g] = decay * st + _dot_tn(bm, x_state[:, g * gw:(g + 1) * gw])
    return jnp.concatenate(ys, axis=1)


def _ssd_kernel(xf_ref, xb_ref, dtf_ref, dtb_ref, a_ref, bias_ref, exp_ref, yf_ref, yb_ref,
                sf_ref, sb_ref, *, nh):
    @pl.when(pl.program_id(1) == 0)
    def _():
        sf_ref[...] = jnp.zeros_like(sf_ref)
        sb_ref[...] = jnp.zeros_like(sb_ref)

    ex = exp_ref[...]
    yf_ref[0] = _ssd_direction(xf_ref[0], dtf_ref[0][:, :nh], a_ref[0:1, :], bias_ref[0:1, :],
                               ex, sf_ref, False)
    yb_ref[0] = _ssd_direction(xb_ref[0], dtb_ref[0][:, nh:2 * nh], a_ref[1:2, :], bias_ref[1:2, :],
                               ex, sb_ref, True)


def ssd_scan(xc, dt_raw, a_log, dt_bias, *, t_lat):
    bsz, s, cw = xc.shape
    nh = a_log.shape[1]
    xw = nh * SSM_HEAD_DIM
    q = SSM_CHUNK
    nt = s // q
    nl = t_lat // q
    nc = nt - nl
    a = -jnp.exp(a_log.astype(F32))
    expand = (jnp.arange(xw)[None, :] // SSM_HEAD_DIM == jnp.arange(nh)[:, None]).astype(BF16)

    def fidx(j):
        return jnp.where(j < nc, nl + j, j - nc)

    def bidx(j):
        return nt - 1 - j

    ndt = dt_raw.shape[-1]
    return pl.pallas_call(
        functools.partial(_ssd_kernel, nh=nh),
        out_shape=(jax.ShapeDtypeStruct((bsz, s, xw), F32), jax.ShapeDtypeStruct((bsz, s, xw), F32)),
        grid=(bsz, nt),
        in_specs=[
            pl.BlockSpec((1, q, cw), lambda b, j: (b, fidx(j), 0)),
            pl.BlockSpec((1, q, cw), lambda b, j: (b, bidx(j), 0)),
            pl.BlockSpec((1, q, ndt), lambda b, j: (b, fidx(j), 0)),
            pl.BlockSpec((1, q, ndt), lambda b, j: (b, bidx(j), 0)),
            pl.BlockSpec((2, nh), lambda b, j: (0, 0)),
            pl.BlockSpec((2, nh), lambda b, j: (0, 0)),
            pl.BlockSpec((nh, xw), lambda b, j: (0, 0)),
        ],
        out_specs=(pl.BlockSpec((1, q, xw), lambda b, j: (b, fidx(j), 0)),
                   pl.BlockSpec((1, q, xw), lambda b, j: (b, bidx(j), 0))),
        scratch_shapes=[pltpu.VMEM((SSM_GROUPS, SSM_STATE, xw // SSM_GROUPS), F32),
                        pltpu.VMEM((SSM_GROUPS, SSM_STATE, xw // SSM_GROUPS), F32)],
        compiler_params=_cparams(("parallel", "arbitrary")),
        name="ssd_scan",
    )(xc, xc, dt_raw, dt_raw, a, dt_bias.astype(F32), expand)


def _ssm_finish_kernel(yf_ref, yb_ref, xs_ref, z_ref, dsk_ref, g_ref, o_ref):
    z = z_ref[0]
    y = (yf_ref[0] + yb_ref[0] + dsk_ref[...] * xs_ref[0]) * (z * jax.nn.sigmoid(z))
    gw = y.shape[1] // SSM_GROUPS
    outs = []
    for g in range(SSM_GROUPS):
        outs.append(_rms(y[:, g * gw:(g + 1) * gw], g_ref[:, g * gw:(g + 1) * gw]))
    o_ref[0] = jnp.concatenate(outs, axis=1).astype(o_ref.dtype)


def ssm_finish(y_f, y_b, xc, p, d_skip, g_norm, *, z_col0, tm):
    bsz, s, xw = y_f.shape
    dsk = jnp.repeat(d_skip.astype(F32), SSM_HEAD_DIM).reshape(1, xw)
    zoff = z_col0 // xw
    spec = pl.BlockSpec((1, tm, xw), lambda b, i: (b, i, 0))
    return pl.pallas_call(
        _ssm_finish_kernel,
        out_shape=jax.ShapeDtypeStruct((bsz, s, xw), BF16),
        grid=(bsz, s // tm),
        in_specs=[spec, spec, spec,
                  pl.BlockSpec((1, tm, xw), lambda b, i: (b, i, zoff)),
                  pl.BlockSpec((1, xw), lambda b, i: (0, 0)),
                  pl.BlockSpec((1, xw), lambda b, i: (0, 0))],
        out_specs=spec,
        compiler_params=_cparams(("parallel", "parallel")),
        name="ssm_finish",
    )(y_f, y_b, xc, p, dsk, g_norm.reshape(1, xw))


def _rope_tables(t_lat, head_dim, reps):
    rows = t_lat // GRID_W
    row = jnp.repeat(jnp.arange(rows, dtype=jnp.int32), GRID_W).astype(F32)
    col = (jnp.arange(t_lat, dtype=jnp.int32) % GRID_W).astype(F32)
    half = head_dim // 2
    inv = ROPE_THETA ** (-(jnp.arange(0, half, 2, dtype=F32) / half))
    ang_r = row[:, None] * inv[None, :]
    ang_c = col[:, None] * inv[None, :]
    cos = jnp.concatenate([jnp.cos(ang_r), jnp.cos(ang_r), jnp.cos(ang_c), jnp.cos(ang_c)], axis=1)
    sin = jnp.concatenate([-jnp.sin(ang_r), jnp.sin(ang_r), -jnp.sin(ang_c), jnp.sin(ang_c)], axis=1)
    return jnp.tile(cos, (1, reps)), jnp.tile(sin, (1, reps))


def _rope(v, cos, sin, quarter):
    w = v.shape[1]
    lane = lax.broadcasted_iota(jnp.int32, v.shape, 1)
    first = (lane % (2 * quarter)) < quarter
    swapped = jnp.where(first, pltpu.roll(v, w - quarter, 1), pltpu.roll(v, quarter, 1))
    return v * cos + swapped * sin


LOG2E = math.log2(math.e)


def _softmax_parts(s2):
    m = jnp.max(s2, axis=-1, keepdims=True)
    e = jnp.exp2(s2 - m)
    return e, jnp.sum(e, axis=-1, keepdims=True)


def _gqa_kernel(q_ref, k_ref, v_ref, cos_ref, sin_ref, cosq_ref, sinq_ref, gq_ref, gk_ref, o_ref,
                kp_ref, vp_ref, *, tq, t_lat, nq_lat):
    qi = pl.program_id(2)
    d = GQA_HEAD_DIM
    quarter = d // 4

    @pl.when(qi == 0)
    def _():
        kn = _rms(k_ref[0], gk_ref[...])
        kp_ref[:t_lat, :] = _rope(kn[:t_lat], cos_ref[...], sin_ref[...], quarter).astype(BF16)
        kp_ref[t_lat:, :] = kn[t_lat:].astype(BF16)
        vp_ref[...] = v_ref[0].astype(BF16)

    scale = d ** -0.5 * LOG2E

    def attend(rope_q, k0):
        for j in range(GQA_GROUP):
            qh = _rms(q_ref[0, :, j * d:(j + 1) * d], gq_ref[...])
            if rope_q:
                qh = _rope(qh, cosq_ref[...], sinq_ref[...], quarter)
            e, l = _softmax_parts(_dot_nt((qh * scale).astype(BF16), kp_ref[k0:, :]))
            o = _dot(e.astype(BF16), vp_ref[k0:, :]) / l
            o_ref[0, :, j * d:(j + 1) * d] = o.astype(o_ref.dtype)

    @pl.when(qi < nq_lat)
    def _():
        attend(True, 0)

    @pl.when(qi >= nq_lat)
    def _():
        attend(False, t_lat)


def gqa_attention(p, g_q, g_k, *, q_col0, k_col0, v_col0, n_kv, t_lat, with_ctx, tq=256):
    bsz, s, _ = p.shape
    d = GQA_HEAD_DIM
    gw = GQA_GROUP * d
    nq_lat = t_lat // tq
    nq = (s if with_ctx else t_lat) // tq
    cos, sin = _rope_tables(t_lat, d, 1)
    qo, ko, vo = q_col0 // gw, k_col0 // d, v_col0 // d
    kern = functools.partial(_gqa_kernel, tq=tq, t_lat=t_lat, nq_lat=nq_lat)
    qtab = lambda b, h, i: (jnp.minimum(i, nq_lat - 1), 0)
    return pl.pallas_call(
        kern,
        out_shape=jax.ShapeDtypeStruct((bsz, nq * tq, n_kv * gw), BF16),
        grid=(bsz, n_kv, nq),
        in_specs=[
            pl.BlockSpec((1, tq, gw), lambda b, h, i: (b, i, qo + h)),
            pl.BlockSpec((1, s, d), lambda b, h, i: (b, 0, ko + h)),
            pl.BlockSpec((1, s, d), lambda b, h, i: (b, 0, vo + h)),
            pl.BlockSpec((t_lat, d), lambda b, h, i: (0, 0)),
            pl.BlockSpec((t_lat, d), lambda b, h, i: (0, 0)),
            pl.BlockSpec((tq, d), qtab),
            pl.BlockSpec((tq, d), qtab),
            pl.BlockSpec((1, d), lambda b, h, i: (0, 0)),
            pl.BlockSpec((1, d), lambda b, h, i: (0, 0)),
        ],
        out_specs=pl.BlockSpec((1, tq, gw), lambda b, h, i: (b, i, h)),
        scratch_shapes=[pltpu.VMEM((s, d), BF16), pltpu.VMEM((s, d), BF16)],
        compiler_params=_cparams(("parallel", "parallel", "arbitrary")),
        name="gqa_attention",
    )(p, p, p, cos, sin, cos, sin, g_q.reshape(1, d), g_k.reshape(1, d))


def _diff_kernel(q_ref, k_ref, v_ref, cos_ref, sin_ref, cosq_ref, sinq_ref, lq_ref, lk_ref, g_ref,
                 o_ref, kp_ref, vp_ref, *, t_lat, nq_lat, lambda_init):
    qi = pl.program_id(2)
    d = DIFF_HEAD_DIM
    quarter = d // 4

    @pl.when(qi == 0)
    def _():
        k = k_ref[0]
        kp_ref[:t_lat, :] = _rope(k[:t_lat], cos_ref[...], sin_ref[...], quarter).astype(BF16)
        kp_ref[t_lat:, :] = k[t_lat:].astype(BF16)
        vp_ref[...] = v_ref[0].astype(BF16)

    prod = lq_ref[...] * lk_ref[...]
    sums = jnp.sum(prod, axis=-1, keepdims=True)
    lam = jnp.exp(sums[0:1]) - jnp.exp(sums[1:2]) + lambda_init
    scale = d ** -0.5 * LOG2E

    def attend(rope_q, k0):
        q = q_ref[0]
        if rope_q:
            q = _rope(q, cosq_ref[...], sinq_ref[...], quarter)
        q = q * scale
        lane = lax.broadcasted_iota(jnp.int32, q.shape, 1)
        kk = kp_ref[k0:, :]
        q0 = jnp.where(lane < d, q, 0.0).astype(BF16)
        q1 = jnp.where(lane >= d, q, 0.0).astype(BF16)
        e0, l0 = _softmax_parts(_dot_nt(q0, kk))
        e1, l1 = _softmax_parts(_dot_nt(q1, kk))
        pd = e0 - (lam * l0 / l1) * e1
        o = _dot(pd.astype(BF16), vp_ref[k0:, :]) / l0
        o_ref[0] = (_rms(o, g_ref[...]) * (1.0 - lambda_init)).astype(o_ref.dtype)

    @pl.when(qi < nq_lat)
    def _():
        attend(True, 0)

    @pl.when(qi >= nq_lat)
    def _():
        attend(False, t_lat)


def diff_attention(p, lambda_q, lambda_k, g_norm, *, q_col0, k_col0, v_col0, n_heads, t_lat,
                   with_ctx, lambda_init, tq=256):
    bsz, s, _ = p.shape
    hw = 2 * DIFF_HEAD_DIM
    nq_lat = t_lat // tq
    nq = (s if with_ctx else t_lat) // tq
    cos, sin = _rope_tables(t_lat, DIFF_HEAD_DIM, 2)
    qo, ko, vo = q_col0 // hw, k_col0 // hw, v_col0 // hw
    kern = functools.partial(_diff_kernel, t_lat=t_lat, nq_lat=nq_lat, lambda_init=lambda_init)
    qtab = lambda b, h, i: (jnp.minimum(i, nq_lat - 1), 0)
    return pl.pallas_call(
        kern,
        out_shape=jax.ShapeDtypeStruct((bsz, nq * tq, n_heads * hw), BF16),
        grid=(bsz, n_heads, nq),
        in_specs=[
            pl.BlockSpec((1, tq, hw), lambda b, h, i: (b, i, qo + h)),
            pl.BlockSpec((1, s, hw), lambda b, h, i: (b, 0, ko + h)),
            pl.BlockSpec((1, s, hw), lambda b, h, i: (b, 0, vo + h)),
            pl.BlockSpec((t_lat, hw), lambda b, h, i: (0, 0)),
            pl.BlockSpec((t_lat, hw), lambda b, h, i: (0, 0)),
            pl.BlockSpec((tq, hw), qtab),
            pl.BlockSpec((tq, hw), qtab),
            pl.BlockSpec((2, DIFF_HEAD_DIM), lambda b, h, i: (0, 0)),
            pl.BlockSpec((2, DIFF_HEAD_DIM), lambda b, h, i: (0, 0)),
            pl.BlockSpec((1, hw), lambda b, h, i: (0, 0)),
        ],
        out_specs=pl.BlockSpec((1, tq, hw), lambda b, h, i: (b, i, h)),
        scratch_shapes=[pltpu.VMEM((s, hw), BF16), pltpu.VMEM((s, hw), BF16)],
        compiler_params=_cparams(("parallel", "parallel", "arbitrary")),
        name="diff_attention",
    )(p, p, p, cos, sin, cos, sin, lambda_q.astype(F32), lambda_k.astype(F32), g_norm.reshape(1, hw))


def _scan_rows(a_ref, b_ref, h_ref, row0, nblk, carry, reverse, accumulate):
    rows = lax.broadcasted_iota(jnp.int32, (SUBLANE, a_ref.shape[1]), 0)

    def body(i, carry):
        blk = (nblk - 1 - i) if reverse else i
        r0 = pl.multiple_of(row0 + blk * SUBLANE, SUBLANE)
        a = a_ref[pl.ds(r0, SUBLANE), :]
        b = b_ref[pl.ds(r0, SUBLANE), :]
        for sft in (1, 2, 4):
            if reverse:
                a_s = pltpu.roll(a, SUBLANE - sft, 0)
                b_s = pltpu.roll(b, SUBLANE - sft, 0)
                ok = rows < SUBLANE - sft
            else:
                a_s = pltpu.roll(a, sft, 0)
                b_s = pltpu.roll(b, sft, 0)
                ok = rows >= sft
            b = jnp.where(ok, a * b_s + b, b)
            a = jnp.where(ok, a * a_s, a)
        h = b + a * carry
        if accumulate:
            h_ref[pl.ds(r0, SUBLANE), :] += h
        else:
            h_ref[pl.ds(r0, SUBLANE), :] = h
        return h[0:1, :] if reverse else h[SUBLANE - 1:SUBLANE, :]

    return lax.fori_loop(0, nblk, body, carry, unroll=4)


LRU_ROW_CHUNK = 256


def _sigmoid_tanh(x):
    return 0.5 * jnp.tanh(0.5 * x) + 0.5


def _lru_kernel(x_ref, g_ref, cw_ref, cb_ref, wrg_ref, wig_ref, brg_ref, big_ref, lam_ref, o_ref,
                xr_ref, a_ref, b_ref, h_ref, *, t_lat):
    s, w = x_ref.shape[1], x_ref.shape[2]
    xr_ref[...] = _dwconv_rows(x_ref[0], cw_ref[...], cb_ref[...], t_lat)
    n_lat = t_lat // SUBLANE
    n_ctx = (s - t_lat) // SUBLANE
    zero = jnp.zeros((1, w), F32)
    for dr in range(2):
        sp = jax.nn.softplus(-lam_ref[dr:dr + 1, :])

        def gates(r, carry, dr=dr, sp=sp):
            rows = pl.ds(pl.multiple_of(r * LRU_ROW_CHUNK, LRU_ROW_CHUNK), LRU_ROW_CHUNK)
            for c in range(w // LANE):
                cs = slice(c * LANE, (c + 1) * LANE)
                xr = xr_ref[rows, cs]
                xb = xr.astype(BF16)
                rg = _sigmoid_tanh(_dot(xb, wrg_ref[dr, c]) + brg_ref[dr:dr + 1, cs])
                ig = _sigmoid_tanh(_dot(xb, wig_ref[dr, c]) + big_ref[dr:dr + 1, cs])
                log_a = -LRU_C * rg * sp[:, cs]
                a = jnp.exp(log_a)
                a_ref[rows, cs] = a
                b_ref[rows, cs] = jnp.sqrt(-jnp.tanh(log_a) * (a * a + 1.0)) * (ig * xr)
            return carry

        lax.fori_loop(0, s // LRU_ROW_CHUNK, gates, 0)
        if dr == 0:
            c = _scan_rows(a_ref, b_ref, h_ref, t_lat, n_ctx, zero, False, False)
            _scan_rows(a_ref, b_ref, h_ref, 0, n_lat, c, False, False)
        else:
            c = _scan_rows(a_ref, b_ref, h_ref, t_lat, n_ctx, zero, True, True)
            _scan_rows(a_ref, b_ref, h_ref, 0, n_lat, c, True, True)

    def finish(r, carry):
        rows = pl.ds(pl.multiple_of(r * LRU_ROW_CHUNK, LRU_ROW_CHUNK), LRU_ROW_CHUNK)
        o_ref[0, rows, :] = (jax.nn.gelu(g_ref[0, rows, :]) * h_ref[rows, :]).astype(o_ref.dtype)
        return carry

    lax.fori_loop(0, s // LRU_ROW_CHUNK, finish, 0)


def rglru(p, conv_w, conv_b, w_rg, b_rg, w_ig, b_ig, lam, *, g_col0, x_col0, t_lat, cw=512):
    bsz, s, _ = p.shape
    width = conv_w.shape[1]
    ncg = width // cw
    nsub = cw // LANE
    go, xo = g_col0 // cw, x_col0 // cw

    def pair_blocks(w):
        nb = w.shape[1]
        w = w.reshape(2, nb // 2, 2, LRU_BLOCK_DIM, LRU_BLOCK_DIM)
        z = jnp.zeros_like(w[:, :, 0])
        top = jnp.concatenate([w[:, :, 0], z], axis=-1)
        bot = jnp.concatenate([z, w[:, :, 1]], axis=-1)
        return jnp.concatenate([top, bot], axis=-2).astype(BF16)

    wspec = pl.BlockSpec((2, nsub, LANE, LANE), lambda b, c: (0, c, 0, 0))
    vspec = pl.BlockSpec((2, cw), lambda b, c: (0, c))
    return pl.pallas_call(
        functools.partial(_lru_kernel, t_lat=t_lat),
        out_shape=jax.ShapeDtypeStruct((bsz, s, width), BF16),
        grid=(bsz, ncg),
        in_specs=[
            pl.BlockSpec((1, s, cw), lambda b, c: (b, 0, xo + c)),
            pl.BlockSpec((1, s, cw), lambda b, c: (b, 0, go + c)),
            pl.BlockSpec((CONV_K, cw), lambda b, c: (0, c)),
            pl.BlockSpec((1, cw), lambda b, c: (0, c)),
            wspec, wspec, vspec, vspec, vspec,
        ],
        out_specs=pl.BlockSpec((1, s, cw), lambda b, c: (b, 0, c)),
        scratch_shapes=[pltpu.VMEM((s, cw), F32)] * 4,
        compiler_params=_cparams(("parallel", "parallel")),
        name="rglru",
    )(p, p, conv_w.astype(F32), conv_b.reshape(1, width).astype(F32), pair_blocks(w_rg), pair_blocks(w_ig),
      b_rg.astype(F32), b_ig.astype(F32), lam.astype(F32))


def _split_w_in(w_in, mix_w, n_ssm_heads):
    gn = SSM_GROUPS * SSM_STATE
    kv = mix_w // GQA_GROUP
    sizes = (mix_w, mix_w + 2 * gn, 2 * n_ssm_heads, mix_w, mix_w, mix_w, mix_w, kv, kv, mix_w, mix_w)
    names = ("z", "xbc", "dt", "q_d", "k_d", "v_d", "q_g", "k_g", "v_g", "g_lru", "x_lru")
    parts = {}
    o = 0
    for nme, sz in zip(names, sizes):
        parts[nme] = w_in[:, o:o + sz]
        o += sz
    order = ("q_g", "q_d", "k_d", "v_d", "z", "g_lru", "x_lru", "xbc", "k_g", "v_g")
    offs = {}
    o = 0
    for nme in order:
        offs[nme] = o
        o += parts[nme].shape[1]
    w_main = jnp.concatenate([parts[nme] for nme in order], axis=1).astype(BF16)
    w_dt = jnp.pad(parts["dt"], ((0, 0), (0, LANE - 2 * n_ssm_heads))).astype(BF16)
    return w_main, w_dt, offs


def kernel(x, c, ctx, c_ctx, w_ada, b_ada, g_pre, g_post, w_ffn_gate, w_ffn_up, w_ffn_down, w_in, conv_w_ssm, conv_b_ssm, a_log, dt_bias, d_skip, g_ssm_norm, lambda_q, lambda_k, g_diff_norm, g_q_norm, g_k_norm, conv_w_lru, conv_b_lru, w_rg, b_rg, w_ig, b_ig, lru_lambda, w_branch, w_gate, w_out):
    bsz, t_lat, d = x.shape
    depth = w_ada.shape[0]
    mix_w = d // 2
    n_ssm_heads = a_log.shape[-1]
    s = t_lat + ctx.shape[1]
    tiles = {"ffn": s // 3, "proj": s // 2, "merge": s // 4, "finish": s // 4, "lat": t_lat // 4}
    assert all(t % ROW_CHUNK == 0 for t in tiles.values()) and t_lat % ROW_CHUNK == 0, tiles

    r = ((bsz + 1 + 15) // 16) * 16
    cc = jnp.zeros((r, d), F32).at[:bsz].set(c).at[bsz].set(c_ctx)
    mods = adaln(cc, w_ada, b_ada)

    h = jnp.concatenate([x, ctx], axis=1)
    for l in range(depth):
        last = l == depth - 1
        lambda_init = 0.8 - 0.6 * math.exp(-0.3 * l)
        modl = mods[l, :bsz].reshape(bsz, N_MOD, d)
        modc = mods[l, bsz].reshape(N_MOD, d)
        w_main, w_dt, offs = _split_w_in(w_in[l], mix_w, n_ssm_heads)

        h = ffn_sublayer(h, modl, modc, g_pre[l, 0], g_post[l, 0],
                         w_ffn_gate[l, 0].astype(BF16), w_ffn_up[l, 0].astype(BF16),
                         w_ffn_down[l, 0].astype(BF16), base=0, t_lat=t_lat, rows=s, tm=tiles["ffn"])

        p, dt_raw = in_projection(h, modl, modc, g_pre[l, 1], w_main, w_dt, base=3, t_lat=t_lat,
                                  tm=tiles["proj"], tn=1024)

        xc = ssm_conv(p, conv_w_ssm[l].astype(F32), conv_b_ssm[l].astype(F32), col0=offs["xbc"], t_lat=t_lat)
        y_f, y_b = ssd_scan(xc, dt_raw, a_log[l], dt_bias[l], t_lat=t_lat)
        ssm = ssm_finish(y_f, y_b, xc, p, d_skip[l], g_ssm_norm[l], z_col0=offs["z"], tm=tiles["finish"])

        dif = diff_attention(p, lambda_q[l], lambda_k[l], g_diff_norm[l], q_col0=offs["q_d"],
                             k_col0=offs["k_d"], v_col0=offs["v_d"], n_heads=mix_w // (2 * DIFF_HEAD_DIM),
                             t_lat=t_lat, with_ctx=not last, lambda_init=lambda_init)
        gqa = gqa_attention(p, g_q_norm[l], g_k_norm[l], q_col0=offs["q_g"], k_col0=offs["k_g"],
                            v_col0=offs["v_g"], n_kv=mix_w // (GQA_GROUP * GQA_HEAD_DIM), t_lat=t_lat,
                            with_ctx=not last)
        lru = rglru(p, conv_w_lru[l], conv_b_lru[l], w_rg[l], b_rg[l], w_ig[l], b_ig[l], lru_lambda[l],
                    g_col0=offs["g_lru"], x_col0=offs["x_lru"], t_lat=t_lat)

        rows = t_lat if last else s
        h = merge_sublayer(h, modl, modc, g_pre[l, 1], g_post[l, 1], (ssm, dif, gqa, lru),
                           w_gate[l].astype(BF16), w_branch[l].astype(BF16), w_out[l].astype(BF16),
                           base=3, t_lat=t_lat, rows=rows, tm=tiles["lat" if last else "merge"])
        h = ffn_sublayer(h, modl, modc, g_pre[l, 2], g_post[l, 2],
                         w_ffn_gate[l, 1].astype(BF16), w_ffn_up[l, 1].astype(BF16),
                         w_ffn_down[l, 1].astype(BF16), base=6, t_lat=t_lat, rows=rows, tm=tiles["lat" if last else "ffn"])
    return h
```

```python
import functools
import math

import jax
import jax.numpy as jnp
from jax import lax
from jax.experimental import pallas as pl
from jax.experimental.pallas import tpu as pltpu

F32 = jnp.float32
BF16 = jnp.bfloat16

GRID_W = 64
N_MOD = 9
FFN_HALF = 0.5
EPS = 1e-6
ROPE_THETA = 10000.0
CONV_K = 4
SSM_HEAD_DIM = 64
SSM_GROUPS = 2
SSM_STATE = 128
SSM_CHUNK = 128
DIFF_HEAD_DIM = 64
GQA_HEAD_DIM = 128
GQA_GROUP = 4
LRU_BLOCK_DIM = 64
LRU_C = 8.0

LANE = 128
SUBLANE = 8
V7X_VMEM_BYTES = 64 * 1024 * 1024
VMEM_LIMIT = V7X_VMEM_BYTES - 6 * 1024 * 1024


def _cparams(sem):
    return pltpu.CompilerParams(dimension_semantics=sem, vmem_limit_bytes=VMEM_LIMIT)


def _dot(a, b):
    return jnp.dot(a, b, preferred_element_type=F32)


def _dot_nt(a, b):
    return lax.dot_general(a, b, (((1,), (1,)), ((), ())), preferred_element_type=F32)


def _dot_tn(a, b):
    return lax.dot_general(a, b, (((0,), (0,)), ((), ())), preferred_element_type=F32)


def _rms(x, g):
    return x * lax.rsqrt(jnp.mean(x * x, axis=-1, keepdims=True) + EPS) * g


def _split_dot(a_f32, b_bf16):
    hi = a_f32.astype(BF16)
    lo = (a_f32 - hi.astype(F32)).astype(BF16)
    return _dot(hi, b_bf16) + _dot(lo, b_bf16)


def _adaln_kernel(c_ref, w_ref, b_ref, o_ref):
    x = c_ref[...]
    x = (x * jax.nn.sigmoid(x)).astype(BF16)
    o_ref[0] = _dot(x, w_ref[0].astype(BF16)) + b_ref[0]


def adaln(cc, w_ada, b_ada, *, tn=1024):
    depth, d, n = w_ada.shape
    r = cc.shape[0]
    return pl.pallas_call(
        _adaln_kernel,
        out_shape=jax.ShapeDtypeStruct((depth, r, n), F32),
        grid=(depth, n // tn),
        in_specs=[
            pl.BlockSpec((r, d), lambda l, j: (0, 0)),
            pl.BlockSpec((1, d, tn), lambda l, j: (l, 0, j)),
            pl.BlockSpec((1, 1, tn), lambda l, j: (l, 0, j)),
        ],
        out_specs=pl.BlockSpec((1, r, tn), lambda l, j: (l, 0, j)),
        compiler_params=_cparams(("parallel", "parallel")),
        name="adaln",
    )(cc, w_ada, b_ada.reshape(depth, 1, n))


ROW_CHUNK = 16
COL_CHUNK = 512


def _for_chunks(lo, hi, body, unroll):
    if hi <= lo:
        return

    def step(r, carry):
        body(pl.ds(pl.multiple_of(r * ROW_CHUNK, ROW_CHUNK), ROW_CHUNK))
        return carry

    lax.fori_loop(lo, hi, step, 0, unroll=min(unroll, hi - lo))


def _segments(i, tm, rows, t_lat, fn):
    n = tm // ROW_CHUNK
    if rows <= t_lat:
        fn(0, n, False)
        return
    last = rows // tm - 1
    assert last * tm <= t_lat, (tm, rows, t_lat)
    nb = (t_lat - last * tm) // ROW_CHUNK

    @pl.when(i < last)
    def _():
        fn(0, n, False)

    @pl.when(i == last)
    def _():
        fn(0, nb, False)
        fn(nb, n, True)


def _row_stats(read, inv_ref, tm):
    def stats(sl):
        x = read(sl)
        inv = lax.rsqrt(jnp.mean(x * x, axis=-1, keepdims=True) + EPS)
        inv_ref[sl, :] = jnp.broadcast_to(inv, (ROW_CHUNK, LANE))

    _for_chunks(0, tm // ROW_CHUNK, stats, 8)


def _lanes(inv_tile, width):
    return jnp.concatenate([inv_tile] * (width // LANE), axis=1)


def _prologue(i, h_ref, n_ref, o_ref, inv_ref, gpre_ref, modl_ref, modc_ref, base, tm, rows, t_lat):
    d = h_ref.shape[-1]
    _row_stats(lambda sl: h_ref[0, sl, :], inv_ref, tm)
    g = gpre_ref[...]

    def seg(lo, hi, is_ctx):
        mod = (lambda j: modc_ref[j:j + 1, :]) if is_ctx else (lambda j: modl_ref[0, j:j + 1, :])
        shift = mod(base)
        gain = g * (1.0 + mod(base + 1))

        def apply(sl):
            n_ref[sl, :] = (h_ref[0, sl, :] * _lanes(inv_ref[sl, :], d) * gain + shift).astype(BF16)
            if o_ref is not None:
                o_ref[0, sl, :] = jnp.zeros((ROW_CHUNK, d), F32)

        _for_chunks(lo, hi, apply, 4)

    _segments(i, tm, rows, t_lat, seg)


def _accumulate(o_ref, a, w_ref):
    for c0 in range(0, o_ref.shape[-1], COL_CHUNK):
        o_ref[0, :, c0:c0 + COL_CHUNK] += _dot(a, w_ref[:, c0:c0 + COL_CHUNK])


def _epilogue(i, h_ref, o_ref, inv_ref, gpost_ref, modl_ref, modc_ref, gate_idx, factor, tm, rows, t_lat):
    d = h_ref.shape[-1]
    _row_stats(lambda sl: o_ref[0, sl, :], inv_ref, tm)
    g = gpost_ref[...] * factor

    def seg(lo, hi, is_ctx):
        gain = g * (modc_ref[gate_idx:gate_idx + 1, :] if is_ctx else modl_ref[0, gate_idx:gate_idx + 1, :])

        def apply(sl):
            o_ref[0, sl, :] = h_ref[0, sl, :] + o_ref[0, sl, :] * _lanes(inv_ref[sl, :], d) * gain

        _for_chunks(lo, hi, apply, 4)

    _segments(i, tm, rows, t_lat, seg)


def _ffn_kernel(h_ref, modl_ref, modc_ref, gpre_ref, gpost_ref, wg_ref, wu_ref, wd_ref,
                o_ref, n_ref, inv_ref, *, tm, rows, t_lat, base, nk):
    i = pl.program_id(1)
    k = pl.program_id(2)

    @pl.when(k == 0)
    def _():
        _prologue(i, h_ref, n_ref, o_ref, inv_ref, gpre_ref, modl_ref, modc_ref, base, tm, rows, t_lat)

    n = n_ref[...]
    g = _dot(n, wg_ref[...])
    u = _dot(n, wu_ref[...])
    a = (g * jax.nn.sigmoid(g) * u).astype(BF16)
    _accumulate(o_ref, a, wd_ref)

    @pl.when(k == nk - 1)
    def _():
        _epilogue(i, h_ref, o_ref, inv_ref, gpost_ref, modl_ref, modc_ref, base + 2, FFN_HALF, tm, rows, t_lat)


def ffn_sublayer(h, modl, modc, g_pre, g_post, wg, wu, wd, *, base, t_lat, rows, tm, tf=512):
    bsz, _, d = h.shape
    dff = wg.shape[1]
    nk = dff // tf
    kern = functools.partial(_ffn_kernel, tm=tm, rows=rows, t_lat=t_lat, base=base, nk=nk)
    return pl.pallas_call(
        kern,
        out_shape=jax.ShapeDtypeStruct((bsz, rows, d), F32),
        grid=(bsz, rows // tm, nk),
        in_specs=[
            pl.BlockSpec((1, tm, d), lambda b, i, k: (b, i, 0)),
            pl.BlockSpec((1, N_MOD, d), lambda b, i, k: (b, 0, 0)),
            pl.BlockSpec((N_MOD, d), lambda b, i, k: (0, 0)),
            pl.BlockSpec((1, d), lambda b, i, k: (0, 0)),
            pl.BlockSpec((1, d), lambda b, i, k: (0, 0)),
            pl.BlockSpec((d, tf), lambda b, i, k: (0, k)),
            pl.BlockSpec((d, tf), lambda b, i, k: (0, k)),
            pl.BlockSpec((tf, d), lambda b, i, k: (k, 0)),
        ],
        out_specs=pl.BlockSpec((1, tm, d), lambda b, i, k: (b, i, 0)),
        scratch_shapes=[pltpu.VMEM((tm, d), BF16), pltpu.VMEM((tm, LANE), F32)],
        compiler_params=_cparams(("parallel", "parallel", "arbitrary")),
        name="ffn_sublayer",
    )(h, modl, modc, g_pre.reshape(1, d), g_post.reshape(1, d), wg, wu, wd)


def _proj_kernel(h_ref, modl_ref, modc_ref, gpre_ref, w_ref, wdt_ref, o_ref, odt_ref, n_ref, inv_ref,
                 *, tm, rows, t_lat, base):
    i = pl.program_id(1)
    j = pl.program_id(2)

    @pl.when(j == 0)
    def _():
        _prologue(i, h_ref, n_ref, None, inv_ref, gpre_ref, modl_ref, modc_ref, base, tm, rows, t_lat)
        odt_ref[0] = _dot(n_ref[...], wdt_ref[...])

    o_ref[0] = _dot(n_ref[...], w_ref[...])


def in_projection(h, modl, modc, g_pre, w_main, w_dt, *, base, t_lat, tm, tn=512):
    bsz, s, d = h.shape
    ncol = w_main.shape[1]
    ndt = w_dt.shape[1]
    kern = functools.partial(_proj_kernel, tm=tm, rows=s, t_lat=t_lat, base=base)
    return pl.pallas_call(
        kern,
        out_shape=(jax.ShapeDtypeStruct((bsz, s, ncol), F32),
                   jax.ShapeDtypeStruct((bsz, s, ndt), F32)),
        grid=(bsz, s // tm, ncol // tn),
        in_specs=[
            pl.BlockSpec((1, tm, d), lambda b, i, j: (b, i, 0)),
            pl.BlockSpec((1, N_MOD, d), lambda b, i, j: (b, 0, 0)),
            pl.BlockSpec((N_MOD, d), lambda b, i, j: (0, 0)),
            pl.BlockSpec((1, d), lambda b, i, j: (0, 0)),
            pl.BlockSpec((d, tn), lambda b, i, j: (0, j)),
            pl.BlockSpec((d, ndt), lambda b, i, j: (0, 0)),
        ],
        out_specs=(pl.BlockSpec((1, tm, tn), lambda b, i, j: (b, i, j)),
                   pl.BlockSpec((1, tm, ndt), lambda b, i, j: (b, i, 0))),
        scratch_shapes=[pltpu.VMEM((tm, d), BF16), pltpu.VMEM((tm, LANE), F32)],
        compiler_params=_cparams(("parallel", "parallel", "arbitrary")),
        name="in_projection",
    )(h, modl, modc, g_pre.reshape(1, d), w_main, w_dt)


def _merge_kernel(h_ref, modl_ref, modc_ref, gpre_ref, gpost_ref, b0_ref, b1_ref, b2_ref, b3_ref,
                  wg_ref, wb_ref, wo_ref, o_ref, n_ref, inv_ref, *, tm, rows, t_lat, base, nk):
    i = pl.program_id(1)
    k = pl.program_id(2)

    @pl.when(k == 0)
    def _():
        _prologue(i, h_ref, n_ref, o_ref, inv_ref, gpre_ref, modl_ref, modc_ref, base, tm, rows, t_lat)

    n = n_ref[...]
    y = None
    for bi, br_ref in enumerate((b0_ref, b1_ref, b2_ref, b3_ref)):
        t = jax.nn.sigmoid(_dot(n, wg_ref[bi])) * _dot(br_ref[0], wb_ref[bi])
        y = t if y is None else y + t
    _accumulate(o_ref, y.astype(BF16), wo_ref)

    @pl.when(k == nk - 1)
    def _():
        _epilogue(i, h_ref, o_ref, inv_ref, gpost_ref, modl_ref, modc_ref, base + 2, 1.0, tm, rows, t_lat)


def merge_sublayer(h, modl, modc, g_pre, g_post, branches, wg, wb, wo, *, base, t_lat, rows, tm, tn=256):
    bsz, _, d = h.shape
    w = branches[0].shape[-1]
    nk = d // tn
    kern = functools.partial(_merge_kernel, tm=tm, rows=rows, t_lat=t_lat, base=base, nk=nk)
    br_spec = pl.BlockSpec((1, tm, w), lambda b, i, k: (b, i, 0))
    return pl.pallas_call(
        kern,
        out_shape=jax.ShapeDtypeStruct((bsz, rows, d), F32),
        grid=(bsz, rows // tm, nk),
        in_specs=[
            pl.BlockSpec((1, tm, d), lambda b, i, k: (b, i, 0)),
            pl.BlockSpec((1, N_MOD, d), lambda b, i, k: (b, 0, 0)),
            pl.BlockSpec((N_MOD, d), lambda b, i, k: (0, 0)),
            pl.BlockSpec((1, d), lambda b, i, k: (0, 0)),
            pl.BlockSpec((1, d), lambda b, i, k: (0, 0)),
            br_spec, br_spec, br_spec, br_spec,
            pl.BlockSpec((4, d, tn), lambda b, i, k: (0, 0, k)),
            pl.BlockSpec((4, w, tn), lambda b, i, k: (0, 0, k)),
            pl.BlockSpec((tn, d), lambda b, i, k: (k, 0)),
        ],
        out_specs=pl.BlockSpec((1, tm, d), lambda b, i, k: (b, i, 0)),
        scratch_shapes=[pltpu.VMEM((tm, d), BF16), pltpu.VMEM((tm, LANE), F32)],
        compiler_params=_cparams(("parallel", "parallel", "arbitrary")),
        name="merge_sublayer",
    )(h, modl, modc, g_pre.reshape(1, d), g_post.reshape(1, d), *branches, wg, wb, wo)


def _dwconv_rows(x, w, b, t_lat):
    s = x.shape[0]
    t = lax.broadcasted_iota(jnp.int32, (s, 1), 0)
    tl = jnp.where(t >= t_lat, t - t_lat, t)
    seg_len = jnp.where(t >= t_lat, s - t_lat, t_lat)
    y = x * w[1:2, :] + b
    for j, d in ((0, -1), (2, 1), (3, 2)):
        xs = pltpu.roll(x, (-d) % s, 0)
        ok = (tl + d >= 0) & (tl + d < seg_len)
        y = y + jnp.where(ok, xs, 0.0) * w[j:j + 1, :]
    return y


def _conv_silu_kernel(x_ref, w_ref, b_ref, o_ref, *, t_lat):
    y = _dwconv_rows(x_ref[0], w_ref[...], b_ref[...], t_lat)
    o_ref[0] = y * jax.nn.sigmoid(y)


def ssm_conv(p, conv_w, conv_b, *, col0, t_lat, tc=512):
    bsz, s, _ = p.shape
    c = conv_w.shape[1]
    off = col0 // tc
    return pl.pallas_call(
        functools.partial(_conv_silu_kernel, t_lat=t_lat),
        out_shape=jax.ShapeDtypeStruct((bsz, s, c), F32),
        grid=(bsz, c // tc),
        in_specs=[
            pl.BlockSpec((1, s, tc), lambda b, j: (b, 0, off + j)),
            pl.BlockSpec((CONV_K, tc), lambda b, j: (0, j)),
            pl.BlockSpec((1, tc), lambda b, j: (0, j)),
        ],
        out_specs=pl.BlockSpec((1, s, tc), lambda b, j: (b, 0, j)),
        compiler_params=_cparams(("parallel", "parallel")),
        name="ssm_conv",
    )(p, conv_w, conv_b.reshape(1, c))


def _ssd_direction(xc, dtraw, a, dtb, expand, state_ref, reverse):
    q = xc.shape[0]
    nh = dtraw.shape[1]
    xw = nh * SSM_HEAD_DIM
    gw = xw // SSM_GROUPS
    hg = nh // SSM_GROUPS
    x = xc[:, :xw]
    dt = jax.nn.softplus(dtraw + dtb)
    d = dt * a
    li = lax.broadcasted_iota(jnp.int32, (q, q), 0)
    si = lax.broadcasted_iota(jnp.int32, (q, q), 1)
    mask = (si >= li) if reverse else (si <= li)
    tri = jnp.where(mask, 1.0, 0.0).astype(BF16)
    tri_t = jnp.where((li >= si) if reverse else (li <= si), 1.0, 0.0).astype(BF16)
    hi = d.astype(BF16)
    mid = (d - hi.astype(F32)).astype(BF16)
    lo = (d - hi.astype(F32) - mid.astype(F32)).astype(BF16)
    cs = _dot(tri, hi) + _dot(tri, mid) + _dot(tri, lo)
    cs_t = _dot_tn(hi, tri_t) + _dot_tn(mid, tri_t) + _dot_tn(lo, tri_t)
    last = 0 if reverse else q - 1
    total = cs[last:last + 1, :]
    w_state = jnp.exp(total - cs) * dt
    e_off = jnp.exp(cs)
    ex = _split_dot(jnp.concatenate([w_state, dt, e_off], axis=0), expand)
    x_state = (x * ex[:q]).astype(BF16)
    x_dt = x * ex[q:2 * q]
    e_off_x = ex[2 * q:]
    lane = lax.broadcasted_iota(jnp.int32, (q, LANE), 1)
    ys = []
    for g in range(SSM_GROUPS):
        bm = xc[:, xw + g * SSM_STATE: xw + (g + 1) * SSM_STATE].astype(BF16)
        cm = xc[:, xw + (SSM_GROUPS + g) * SSM_STATE: xw + (SSM_GROUPS + g + 1) * SSM_STATE].astype(BF16)
        cb = _dot_nt(cm, bm)
        st = state_ref[g]
        y_off = _dot(cm, st.astype(BF16)) * e_off_x[:, g * gw:(g + 1) * gw]
        yd = []
        for hp in range(hg // 2):
            ms = []
            for hh in range(2):
                h = g * hg + hp * 2 + hh
                seg = cs[:, h:h + 1] - cs_t[h:h + 1, :]
                ms.append((cb * jnp.where(mask, jnp.exp(seg), 0.0)).astype(BF16))
            c0 = g * gw + hp * LANE
            xp = x_dt[:, c0:c0 + LANE]
            rhs = jnp.concatenate([jnp.where(lane < SSM_HEAD_DIM, xp, 0.0).astype(BF16),
                                   jnp.where(lane >= SSM_HEAD_DIM, xp, 0.0).astype(BF16)], axis=0)
            yd.append(_dot(jnp.concatenate(ms, axis=1), rhs))
        ys.append(jnp.concatenate(yd, axis=1) + y_off)
        decay = e_off_x[last:last + 1, g * gw:(g + 1) * gw]
        state_ref[g] = decay * st + _dot_tn(bm, x_state[:, g * gw:(g + 1) * gw])
    return jnp.concatenate(ys, axis=1)


def _ssd_kernel(xf_ref, xb_ref, dtf_ref, dtb_ref, a_ref, bias_ref, exp_ref, yf_ref, yb_ref,
                sf_ref, sb_ref, *, nh):
    @pl.when(pl.program_id(1) == 0)
    def _():
        sf_ref[...] = jnp.zeros_like(sf_ref)
        sb_ref[...] = jnp.zeros_like(sb_ref)

    ex = exp_ref[...]
    yf_ref[0] = _ssd_direction(xf_ref[0], dtf_ref[0][:, :nh], a_ref[0:1, :], bias_ref[0:1, :],
                               ex, sf_ref, False)
    yb_ref[0] = _ssd_direction(xb_ref[0], dtb_ref[0][:, nh:2 * nh], a_ref[1:2, :], bias_ref[1:2, :],
                               ex, sb_ref, True)


def ssd_scan(xc, dt_raw, a_log, dt_bias, *, t_lat):
    bsz, s, cw = xc.shape
    nh = a_log.shape[1]
    xw = nh * SSM_HEAD_DIM
    q = SSM_CHUNK
    nt = s // q
    nl = t_lat // q
    nc = nt - nl
    a = -jnp.exp(a_log.astype(F32))
    expand = (jnp.arange(xw)[None, :] // SSM_HEAD_DIM == jnp.arange(nh)[:, None]).astype(BF16)

    def fidx(j):
        return jnp.where(j < nc, nl + j, j - nc)

    def bidx(j):
        return nt - 1 - j

    ndt = dt_raw.shape[-1]
    return pl.pallas_call(
        functools.partial(_ssd_kernel, nh=nh),
        out_shape=(jax.ShapeDtypeStruct((bsz, s, xw), F32), jax.ShapeDtypeStruct((bsz, s, xw), F32)),
        grid=(bsz, nt),
        in_specs=[
            pl.BlockSpec((1, q, cw), lambda b, j: (b, fidx(j), 0)),
            pl.BlockSpec((1, q, cw), lambda b, j: (b, bidx(j), 0)),
            pl.BlockSpec((1, q, ndt), lambda b, j: (b, fidx(j), 0)),
            pl.BlockSpec((1, q, ndt), lambda b, j: (b, bidx(j), 0)),
            pl.BlockSpec((2, nh), lambda b, j: (0, 0)),
            pl.BlockSpec((2, nh), lambda b, j: (0, 0)),
            pl.BlockSpec((nh, xw), lambda b, j: (0, 0)),
        ],
        out_specs=(pl.BlockSpec((1, q, xw), lambda b, j: (b, fidx(j), 0)),
                   pl.BlockSpec((1, q, xw), lambda b, j: (b, bidx(j), 0))),
        scratch_shapes=[pltpu.VMEM((SSM_GROUPS, SSM_STATE, xw // SSM_GROUPS), F32),
                        pltpu.VMEM((SSM_GROUPS, SSM_STATE, xw // SSM_GROUPS), F32)],
        compiler_params=_cparams(("parallel", "arbitrary")),
        name="ssd_scan",
    )(xc, xc, dt_raw, dt_raw, a, dt_bias.astype(F32), expand)


def _ssm_finish_kernel(yf_ref, yb_ref, xs_ref, z_ref, dsk_ref, g_ref, o_ref):
    z = z_ref[0]
    y = (yf_ref[0] + yb_ref[0] + dsk_ref[...] * xs_ref[0]) * (z * jax.nn.sigmoid(z))
    gw = y.shape[1] // SSM_GROUPS
    outs = []
    for g in range(SSM_GROUPS):
        outs.append(_rms(y[:, g * gw:(g + 1) * gw], g_ref[:, g * gw:(g + 1) * gw]))
    o_ref[0] = jnp.concatenate(outs, axis=1).astype(o_ref.dtype)


def ssm_finish(y_f, y_b, xc, p, d_skip, g_norm, *, z_col0, tm):
    bsz, s, xw = y_f.shape
    dsk = jnp.repeat(d_skip.astype(F32), SSM_HEAD_DIM).reshape(1, xw)
    zoff = z_col0 // xw
    spec = pl.BlockSpec((1, tm, xw), lambda b, i: (b, i, 0))
    return pl.pallas_call(
        _ssm_finish_kernel,
        out_shape=jax.ShapeDtypeStruct((bsz, s, xw), BF16),
        grid=(bsz, s // tm),
        in_specs=[spec, spec, spec,
                  pl.BlockSpec((1, tm, xw), lambda b, i: (b, i, zoff)),
                  pl.BlockSpec((1, xw), lambda b, i: (0, 0)),
                  pl.BlockSpec((1, xw), lambda b, i: (0, 0))],
        out_specs=spec,
        compiler_params=_cparams(("parallel", "parallel")),
        name="ssm_finish",
    )(y_f, y_b, xc, p, dsk, g_norm.reshape(1, xw))


def _rope_tables(t_lat, head_dim, reps):
    rows = t_lat // GRID_W
    row = jnp.repeat(jnp.arange(rows, dtype=jnp.int32), GRID_W).astype(F32)
    col = (jnp.arange(t_lat, dtype=jnp.int32) % GRID_W).astype(F32)
    half = head_dim // 2
    inv = ROPE_THETA ** (-(jnp.arange(0, half, 2, dtype=F32) / half))
    ang_r = row[:, None] * inv[None, :]
    ang_c = col[:, None] * inv[None, :]
    cos = jnp.concatenate([jnp.cos(ang_r), jnp.cos(ang_r), jnp.cos(ang_c), jnp.cos(ang_c)], axis=1)
    sin = jnp.concatenate([-jnp.sin(ang_r), jnp.sin(ang_r), -jnp.sin(ang_c), jnp.sin(ang_c)], axis=1)
    return jnp.tile(cos, (1, reps)), jnp.tile(sin, (1, reps))


def _rope(v, cos, sin, quarter):
    w = v.shape[1]
    lane = lax.broadcasted_iota(jnp.int32, v.shape, 1)
    first = (lane % (2 * quarter)) < quarter
    swapped = jnp.where(first, pltpu.roll(v, w - quarter, 1), pltpu.roll(v, quarter, 1))
    return v * cos + swapped * sin


LOG2E = math.log2(math.e)


def _softmax_parts(s2):
    m = jnp.max(s2, axis=-1, keepdims=True)
    e = jnp.exp2(s2 - m)
    return e, jnp.sum(e, axis=-1, keepdims=True)


def _gqa_kernel(q_ref, k_ref, v_ref, cos_ref, sin_ref, cosq_ref, sinq_ref, gq_ref, gk_ref, o_ref,
                kp_ref, vp_ref, *, t_lat, nq_lat, kps):
    qi = pl.program_id(2)
    d = GQA_HEAD_DIM
    quarter = d // 4

    @pl.when(qi == 0)
    def _():
        for h in range(kps):
            cs = slice(h * d, (h + 1) * d)
            kn = _rms(k_ref[0, :, cs], gk_ref[...])
            kp_ref[:t_lat, cs] = _rope(kn[:t_lat], cos_ref[...], sin_ref[...], quarter).astype(BF16)
            kp_ref[t_lat:, cs] = kn[t_lat:].astype(BF16)
            vp_ref[:, cs] = v_ref[0, :, cs].astype(BF16)

    scale = d ** -0.5 * LOG2E

    def attend(rope_q, k0):
        for j in range(kps * GQA_GROUP):
            h = j // GQA_GROUP
            qh = _rms(q_ref[0, :, j * d:(j + 1) * d], gq_ref[...])
            if rope_q:
                qh = _rope(qh, cosq_ref[...], sinq_ref[...], quarter)
            e, l = _softmax_parts(_dot_nt((qh * scale).astype(BF16), kp_ref[k0:, h * d:(h + 1) * d]))
            o = _dot(e.astype(BF16), vp_ref[k0:, h * d:(h + 1) * d]) / l
            o_ref[0, :, j * d:(j + 1) * d] = o.astype(o_ref.dtype)

    @pl.when(qi < nq_lat)
    def _():
        attend(True, 0)

    @pl.when(qi >= nq_lat)
    def _():
        attend(False, t_lat)


def gqa_attention(p, g_q, g_k, *, q_col0, k_col0, v_col0, n_kv, t_lat, with_ctx, tq=256, kps=2):
    bsz, s, _ = p.shape
    d = GQA_HEAD_DIM
    gw = kps * GQA_GROUP * d
    kw = kps * d
    nq_lat = t_lat // tq
    nq = (s if with_ctx else t_lat) // tq
    cos, sin = _rope_tables(t_lat, d, 1)
    qo, ko, vo = q_col0 // gw, k_col0 // kw, v_col0 // kw
    kern = functools.partial(_gqa_kernel, t_lat=t_lat, nq_lat=nq_lat, kps=kps)
    qtab = lambda b, h, i: (jnp.minimum(i, nq_lat - 1), 0)
    return pl.pallas_call(
        kern,
        out_shape=jax.ShapeDtypeStruct((bsz, nq * tq, n_kv * GQA_GROUP * d), BF16),
        grid=(bsz, n_kv // kps, nq),
        in_specs=[
            pl.BlockSpec((1, tq, gw), lambda b, h, i: (b, i, qo + h)),
            pl.BlockSpec((1, s, kw), lambda b, h, i: (b, 0, ko + h)),
            pl.BlockSpec((1, s, kw), lambda b, h, i: (b, 0, vo + h)),
            pl.BlockSpec((t_lat, d), lambda b, h, i: (0, 0)),
            pl.BlockSpec((t_lat, d), lambda b, h, i: (0, 0)),
            pl.BlockSpec((tq, d), qtab),
            pl.BlockSpec((tq, d), qtab),
            pl.BlockSpec((1, d), lambda b, h, i: (0, 0)),
            pl.BlockSpec((1, d), lambda b, h, i: (0, 0)),
        ],
        out_specs=pl.BlockSpec((1, tq, gw), lambda b, h, i: (b, i, h)),
        scratch_shapes=[pltpu.VMEM((s, kw), BF16), pltpu.VMEM((s, kw), BF16)],
        compiler_params=_cparams(("parallel", "parallel", "arbitrary")),
        name="gqa_attention",
    )(p, p, p, cos, sin, cos, sin, g_q.reshape(1, d), g_k.reshape(1, d))


def _diff_kernel(q_ref, k_ref, v_ref, cos_ref, sin_ref, cosq_ref, sinq_ref, lq_ref, lk_ref, g_ref,
                 o_ref, kp_ref, vp_ref, *, t_lat, nq_lat, lambda_init, hps):
    qi = pl.program_id(2)
    d = DIFF_HEAD_DIM
    hw = 2 * d
    quarter = d // 4

    @pl.when(qi == 0)
    def _():
        for j in range(hps):
            cs = slice(j * hw, (j + 1) * hw)
            kp_ref[:t_lat, cs] = _rope(k_ref[0, :t_lat, cs], cos_ref[...], sin_ref[...], quarter).astype(BF16)
            kp_ref[t_lat:, cs] = k_ref[0, t_lat:, cs].astype(BF16)
            vp_ref[:, cs] = v_ref[0, :, cs].astype(BF16)

    prod = lq_ref[...] * lk_ref[...]
    sums = jnp.sum(prod, axis=-1, keepdims=True)
    lam = jnp.exp(sums[0:1]) - jnp.exp(sums[1:2]) + lambda_init
    scale = d ** -0.5 * LOG2E

    def attend(rope_q, k0):
        for j in range(hps):
            cs = slice(j * hw, (j + 1) * hw)
            q = q_ref[0, :, cs]
            if rope_q:
                q = _rope(q, cosq_ref[...], sinq_ref[...], quarter)
            q = q * scale
            lane = lax.broadcasted_iota(jnp.int32, q.shape, 1)
            kk = kp_ref[k0:, cs]
            q0 = jnp.where(lane < d, q, 0.0).astype(BF16)
            q1 = jnp.where(lane >= d, q, 0.0).astype(BF16)
            e0, l0 = _softmax_parts(_dot_nt(q0, kk))
            e1, l1 = _softmax_parts(_dot_nt(q1, kk))
            pd = e0 - (lam * l0 / l1) * e1
            o = _dot(pd.astype(BF16), vp_ref[k0:, cs]) / l0
            o_ref[0, :, cs] = (_rms(o, g_ref[...]) * (1.0 - lambda_init)).astype(o_ref.dtype)

    @pl.when(qi < nq_lat)
    def _():
        attend(True, 0)

    @pl.when(qi >= nq_lat)
    def _():
        attend(False, t_lat)


def diff_attention(p, lambda_q, lambda_k, g_norm, *, q_col0, k_col0, v_col0, n_heads, t_lat,
                   with_ctx, lambda_init, tq=256, hps=4):
    bsz, s, _ = p.shape
    hw = 2 * DIFF_HEAD_DIM
    gw = hps * hw
    nq_lat = t_lat // tq
    nq = (s if with_ctx else t_lat) // tq
    cos, sin = _rope_tables(t_lat, DIFF_HEAD_DIM, 2)
    qo, ko, vo = q_col0 // gw, k_col0 // gw, v_col0 // gw
    kern = functools.partial(_diff_kernel, t_lat=t_lat, nq_lat=nq_lat, lambda_init=lambda_init, hps=hps)
    qtab = lambda b, h, i: (jnp.minimum(i, nq_lat - 1), 0)
    return pl.pallas_call(
        kern,
        out_shape=jax.ShapeDtypeStruct((bsz, nq * tq, n_heads * hw), BF16),
        grid=(bsz, n_heads // hps, nq),
        in_specs=[
            pl.BlockSpec((1, tq, gw), lambda b, h, i: (b, i, qo + h)),
            pl.BlockSpec((1, s, gw), lambda b, h, i: (b, 0, ko + h)),
            pl.BlockSpec((1, s, gw), lambda b, h, i: (b, 0, vo + h)),
            pl.BlockSpec((t_lat, hw), lambda b, h, i: (0, 0)),
            pl.BlockSpec((t_lat, hw), lambda b, h, i: (0, 0)),
            pl.BlockSpec((tq, hw), qtab),
            pl.BlockSpec((tq, hw), qtab),
            pl.BlockSpec((2, DIFF_HEAD_DIM), lambda b, h, i: (0, 0)),
            pl.BlockSpec((2, DIFF_HEAD_DIM), lambda b, h, i: (0, 0)),
            pl.BlockSpec((1, hw), lambda b, h, i: (0, 0)),
        ],
        out_specs=pl.BlockSpec((1, tq, gw), lambda b, h, i: (b, i, h)),
        scratch_shapes=[pltpu.VMEM((s, gw), BF16), pltpu.VMEM((s, gw), BF16)],
        compiler_params=_cparams(("parallel", "parallel", "arbitrary")),
        name="diff_attention",
    )(p, p, p, cos, sin, cos, sin, lambda_q.astype(F32), lambda_k.astype(F32), g_norm.reshape(1, hw))


def _scan_rows(a_ref, b_ref, h_ref, row0, nblk, carry, reverse, accumulate):
    rows = lax.broadcasted_iota(jnp.int32, (SUBLANE, a_ref.shape[1]), 0)

    def body(i, carry):
        blk = (nblk - 1 - i) if reverse else i
        r0 = pl.multiple_of(row0 + blk * SUBLANE, SUBLANE)
        a = a_ref[pl.ds(r0, SUBLANE), :]
        b = b_ref[pl.ds(r0, SUBLANE), :]
        for sft in (1, 2, 4):
            if reverse:
                a_s = pltpu.roll(a, SUBLANE - sft, 0)
                b_s = pltpu.roll(b, SUBLANE - sft, 0)
                ok = rows < SUBLANE - sft
            else:
                a_s = pltpu.roll(a, sft, 0)
                b_s = pltpu.roll(b, sft, 0)
                ok = rows >= sft
            b = jnp.where(ok, a * b_s + b, b)
            a = jnp.where(ok, a * a_s, a)
        h = b + a * carry
        if accumulate:
            h_ref[pl.ds(r0, SUBLANE), :] += h
        else:
            h_ref[pl.ds(r0, SUBLANE), :] = h
        return h[0:1, :] if reverse else h[SUBLANE - 1:SUBLANE, :]

    return lax.fori_loop(0, nblk, body, carry, unroll=4)


LRU_ROW_CHUNK = 256


def _sigmoid_tanh(x):
    return 0.5 * jnp.tanh(0.5 * x) + 0.5


def _lru_kernel(x_ref, g_ref, cw_ref, cb_ref, wrg_ref, wig_ref, brg_ref, big_ref, lam_ref, o_ref,
                xr_ref, a_ref, b_ref, h_ref, *, t_lat):
    s, w = x_ref.shape[1], x_ref.shape[2]
    xr_ref[...] = _dwconv_rows(x_ref[0], cw_ref[...], cb_ref[...], t_lat)
    n_lat = t_lat // SUBLANE
    n_ctx = (s - t_lat) // SUBLANE
    zero = jnp.zeros((1, w), F32)
    for dr in range(2):
        sp = jax.nn.softplus(-lam_ref[dr:dr + 1, :])

        def gates(r, carry, dr=dr, sp=sp):
            rows = pl.ds(pl.multiple_of(r * LRU_ROW_CHUNK, LRU_ROW_CHUNK), LRU_ROW_CHUNK)
            for c in range(w // LANE):
                cs = slice(c * LANE, (c + 1) * LANE)
                xr = xr_ref[rows, cs]
                xb = xr.astype(BF16)
                rg = _sigmoid_tanh(_dot(xb, wrg_ref[dr, c]) + brg_ref[dr:dr + 1, cs])
                ig = _sigmoid_tanh(_dot(xb, wig_ref[dr, c]) + big_ref[dr:dr + 1, cs])
                log_a = -LRU_C * rg * sp[:, cs]
                a = jnp.exp(log_a)
                a_ref[rows, cs] = a
                b_ref[rows, cs] = jnp.sqrt(-jnp.tanh(log_a) * (a * a + 1.0)) * (ig * xr)
            return carry

        lax.fori_loop(0, s // LRU_ROW_CHUNK, gates, 0)
        if dr == 0:
            c = _scan_rows(a_ref, b_ref, h_ref, t_lat, n_ctx, zero, False, False)
            _scan_rows(a_ref, b_ref, h_ref, 0, n_lat, c, False, False)
        else:
            c = _scan_rows(a_ref, b_ref, h_ref, t_lat, n_ctx, zero, True, True)
            _scan_rows(a_ref, b_ref, h_ref, 0, n_lat, c, True, True)

    def finish(r, carry):
        rows = pl.ds(pl.multiple_of(r * LRU_ROW_CHUNK, LRU_ROW_CHUNK), LRU_ROW_CHUNK)
        o_ref[0, rows, :] = (jax.nn.gelu(g_ref[0, rows, :]) * h_ref[rows, :]).astype(o_ref.dtype)
        return carry

    lax.fori_loop(0, s // LRU_ROW_CHUNK, finish, 0)


def rglru(p, conv_w, conv_b, w_rg, b_rg, w_ig, b_ig, lam, *, g_col0, x_col0, t_lat, cw=512):
    bsz, s, _ = p.shape
    width = conv_w.shape[1]
    ncg = width // cw
    nsub = cw // LANE
    go, xo = g_col0 // cw, x_col0 // cw

    def pair_blocks(w):
        nb = w.shape[1]
        w = w.reshape(2, nb // 2, 2, LRU_BLOCK_DIM, LRU_BLOCK_DIM)
        z = jnp.zeros_like(w[:, :, 0])
        top = jnp.concatenate([w[:, :, 0], z], axis=-1)
        bot = jnp.concatenate([z, w[:, :, 1]], axis=-1)
        return jnp.concatenate([top, bot], axis=-2).astype(BF16)

    wspec = pl.BlockSpec((2, nsub, LANE, LANE), lambda b, c: (0, c, 0, 0))
    vspec = pl.BlockSpec((2, cw), lambda b, c: (0, c))
    return pl.pallas_call(
        functools.partial(_lru_kernel, t_lat=t_lat),
        out_shape=jax.ShapeDtypeStruct((bsz, s, width), BF16),
        grid=(bsz, ncg),
        in_specs=[
            pl.BlockSpec((1, s, cw), lambda b, c: (b, 0, xo + c)),
            pl.BlockSpec((1, s, cw), lambda b, c: (b, 0, go + c)),
            pl.BlockSpec((CONV_K, cw), lambda b, c: (0, c)),
            pl.BlockSpec((1, cw), lambda b, c: (0, c)),
            wspec, wspec, vspec, vspec, vspec,
        ],
        out_specs=pl.BlockSpec((1, s, cw), lambda b, c: (b, 0, c)),
        scratch_shapes=[pltpu.VMEM((s, cw), F32)] * 4,
        compiler_params=_cparams(("parallel", "parallel")),
        name="rglru",
    )(p, p, conv_w.astype(F32), conv_b.reshape(1, width).astype(F32), pair_blocks(w_rg), pair_blocks(w_ig),
      b_rg.astype(F32), b_ig.astype(F32), lam.astype(F32))


def _split_w_in(w_in, mix_w, n_ssm_heads):
    gn = SSM_GROUPS * SSM_STATE
    kv = mix_w // GQA_GROUP
    sizes = (mix_w, mix_w + 2 * gn, 2 * n_ssm_heads, mix_w, mix_w, mix_w, mix_w, kv, kv, mix_w, mix_w)
    names = ("z", "xbc", "dt", "q_d", "k_d", "v_d", "q_g", "k_g", "v_g", "g_lru", "x_lru")
    parts = {}
    o = 0
    for nme, sz in zip(names, sizes):
        parts[nme] = w_in[:, o:o + sz]
        o += sz
    order = ("q_g", "q_d", "k_d", "v_d", "z", "g_lru", "x_lru", "xbc", "k_g", "v_g")
    offs = {}
    o = 0
    for nme in order:
        offs[nme] = o
        o += parts[nme].shape[1]
    w_main = jnp.concatenate([parts[nme] for nme in order], axis=1).astype(BF16)
    w_dt = jnp.pad(parts["dt"], ((0, 0), (0, LANE - 2 * n_ssm_heads))).astype(BF16)
    return w_main, w_dt, offs


def kernel(x, c, ctx, c_ctx, w_ada, b_ada, g_pre, g_post, w_ffn_gate, w_ffn_up, w_ffn_down, w_in, conv_w_ssm, conv_b_ssm, a_log, dt_bias, d_skip, g_ssm_norm, lambda_q, lambda_k, g_diff_norm, g_q_norm, g_k_norm, conv_w_lru, conv_b_lru, w_rg, b_rg, w_ig, b_ig, lru_lambda, w_branch, w_gate, w_out):
    bsz, t_lat, d = x.shape
    depth = w_ada.shape[0]
    mix_w = d // 2
    n_ssm_heads = a_log.shape[-1]
    s = t_lat + ctx.shape[1]
    tiles = {"ffn": s // 3, "proj": s // 2, "merge": s // 4, "finish": s // 4, "lat": t_lat // 4}
    assert all(t % ROW_CHUNK == 0 for t in tiles.values()) and t_lat % ROW_CHUNK == 0, tiles

    r = ((bsz + 1 + 15) // 16) * 16
    cc = jnp.zeros((r, d), F32).at[:bsz].set(c).at[bsz].set(c_ctx)
    mods = adaln(cc, w_ada, b_ada)

    h = jnp.concatenate([x, ctx], axis=1)
    for l in range(depth):
        last = l == depth - 1
        lambda_init = 0.8 - 0.6 * math.exp(-0.3 * l)
        modl = mods[l, :bsz].reshape(bsz, N_MOD, d)
        modc = mods[l, bsz].reshape(N_MOD, d)
        w_main, w_dt, offs = _split_w_in(w_in[l], mix_w, n_ssm_heads)

        h = ffn_sublayer(h, modl, modc, g_pre[l, 0], g_post[l, 0],
                         w_ffn_gate[l, 0].astype(BF16), w_ffn_up[l, 0].astype(BF16),
                         w_ffn_down[l, 0].astype(BF16), base=0, t_lat=t_lat, rows=s, tm=tiles["ffn"])

        p, dt_raw = in_projection(h, modl, modc, g_pre[l, 1], w_main, w_dt, base=3, t_lat=t_lat,
                                  tm=tiles["proj"], tn=1024)

        xc = ssm_conv(p, conv_w_ssm[l].astype(F32), conv_b_ssm[l].astype(F32), col0=offs["xbc"], t_lat=t_lat)
        y_f, y_b = ssd_scan(xc, dt_raw, a_log[l], dt_bias[l], t_lat=t_lat)
        ssm = ssm_finish(y_f, y_b, xc, p, d_skip[l], g_ssm_norm[l], z_col0=offs["z"], tm=tiles["finish"])

        dif = diff_attention(p, lambda_q[l], lambda_k[l], g_diff_norm[l], q_col0=offs["q_d"],
                             k_col0=offs["k_d"], v_col0=offs["v_d"], n_heads=mix_w // (2 * DIFF_HEAD_DIM),
                             t_lat=t_lat, with_ctx=not last, lambda_init=lambda_init)
        gqa = gqa_attention(p, g_q_norm[l], g_k_norm[l], q_col0=offs["q_g"], k_col0=offs["k_g"],
                            v_col0=offs["v_g"], n_kv=mix_w // (GQA_GROUP * GQA_HEAD_DIM), t_lat=t_lat,
                            with_ctx=not last)
        lru = rglru(p, conv_w_lru[l], conv_b_lru[l], w_rg[l], b_rg[l], w_ig[l], b_ig[l], lru_lambda[l],
                    g_col0=offs["g_lru"], x_col0=offs["x_lru"], t_lat=t_lat)

        rows = t_lat if last else s
        h = merge_sublayer(h, modl, modc, g_pre[l, 1], g_post[l, 1], (ssm, dif, gqa, lru),
                           w_gate[l].astype(BF16), w_branch[l].astype(BF16), w_out[l].astype(BF16),
                           base=3, t_lat=t_lat, rows=rows, tm=tiles["lat" if last else "merge"])
        h = ffn_sublayer(h, modl, modc, g_pre[l, 2], g_post[l, 2],
                         w_ffn_gate[l, 1].astype(BF16), w_ffn_up[l, 1].astype(BF16),
                         w_ffn_down[l, 1].astype(BF16), base=6, t_lat=t_lat, rows=rows, tm=tiles["lat" if last else "ffn"])
    return h
```

```python
import functools
import math

import jax
import jax.numpy as jnp
from jax import lax
from jax.experimental import pallas as pl
from jax.experimental.pallas import tpu as pltpu

F32 = jnp.float32
BF16 = jnp.bfloat16

GRID_W = 64
N_MOD = 9
FFN_HALF = 0.5
EPS = 1e-6
ROPE_THETA = 10000.0
CONV_K = 4
SSM_HEAD_DIM = 64
SSM_GROUPS = 2
SSM_STATE = 128
SSM_CHUNK = 128
DIFF_HEAD_DIM = 64
GQA_HEAD_DIM = 128
GQA_GROUP = 4
LRU_BLOCK_DIM = 64
LRU_C = 8.0

LANE = 128
SUBLANE = 8
V7X_VMEM_BYTES = 64 * 1024 * 1024
VMEM_LIMIT = V7X_VMEM_BYTES - 6 * 1024 * 1024


def _cparams(sem):
    return pltpu.CompilerParams(dimension_semantics=sem, vmem_limit_bytes=VMEM_LIMIT)


def _dot(a, b):
    return jnp.dot(a, b, preferred_element_type=F32)


def _dot_nt(a, b):
    return lax.dot_general(a, b, (((1,), (1,)), ((), ())), preferred_element_type=F32)


def _dot_tn(a, b):
    return lax.dot_general(a, b, (((0,), (0,)), ((), ())), preferred_element_type=F32)


def _rms(x, g):
    return x * lax.rsqrt(jnp.mean(x * x, axis=-1, keepdims=True) + EPS) * g


def _split_dot(a_f32, b_bf16):
    hi = a_f32.astype(BF16)
    lo = (a_f32 - hi.astype(F32)).astype(BF16)
    return _dot(hi, b_bf16) + _dot(lo, b_bf16)


def _adaln_kernel(c_ref, w_ref, b_ref, o_ref):
    x = c_ref[...]
    x = (x * jax.nn.sigmoid(x)).astype(BF16)
    o_ref[0] = _dot(x, w_ref[0].astype(BF16)) + b_ref[0]


def adaln(cc, w_ada, b_ada, *, tn=1024):
    depth, d, n = w_ada.shape
    r = cc.shape[0]
    return pl.pallas_call(
        _adaln_kernel,
        out_shape=jax.ShapeDtypeStruct((depth, r, n), F32),
        grid=(depth, n // tn),
        in_specs=[
            pl.BlockSpec((r, d), lambda l, j: (0, 0)),
            pl.BlockSpec((1, d, tn), lambda l, j: (l, 0, j)),
            pl.BlockSpec((1, 1, tn), lambda l, j: (l, 0, j)),
        ],
        out_specs=pl.BlockSpec((1, r, tn), lambda l, j: (l, 0, j)),
        compiler_params=_cparams(("parallel", "parallel")),
        name="adaln",
    )(cc, w_ada, b_ada.reshape(depth, 1, n))


ROW_CHUNK = 16
COL_CHUNK = 512


def _for_chunks(lo, hi, body, unroll):
    if hi <= lo:
        return

    def step(r, carry):
        body(pl.ds(pl.multiple_of(r * ROW_CHUNK, ROW_CHUNK), ROW_CHUNK))
        return carry

    lax.fori_loop(lo, hi, step, 0, unroll=min(unroll, hi - lo))


def _segments(i, tm, rows, t_lat, fn):
    n = tm // ROW_CHUNK
    if rows <= t_lat:
        fn(0, n, False)
        return
    last = rows // tm - 1
    assert last * tm <= t_lat, (tm, rows, t_lat)
    nb = (t_lat - last * tm) // ROW_CHUNK

    @pl.when(i < last)
    def _():
        fn(0, n, False)

    @pl.when(i == last)
    def _():
        fn(0, nb, False)
        fn(nb, n, True)


def _row_stats(read, inv_ref, tm):
    def stats(sl):
        x = read(sl)
        inv = lax.rsqrt(jnp.mean(x * x, axis=-1, keepdims=True) + EPS)
        inv_ref[sl, :] = jnp.broadcast_to(inv, (ROW_CHUNK, LANE))

    _for_chunks(0, tm // ROW_CHUNK, stats, 16)


def _lanes(inv_tile, width):
    return jnp.concatenate([inv_tile] * (width // LANE), axis=1)


def _prologue(i, h_ref, n_ref, o_ref, inv_ref, gpre_ref, modl_ref, modc_ref, base, tm, rows, t_lat):
    d = h_ref.shape[-1]
    _row_stats(lambda sl: h_ref[0, sl, :], inv_ref, tm)
    g = gpre_ref[...]

    def seg(lo, hi, is_ctx):
        mod = (lambda j: modc_ref[j:j + 1, :]) if is_ctx else (lambda j: modl_ref[0, j:j + 1, :])
        shift = mod(base)
        gain = g * (1.0 + mod(base + 1))

        def apply(sl):
            n_ref[sl, :] = (h_ref[0, sl, :] * _lanes(inv_ref[sl, :], d) * gain + shift).astype(BF16)
            if o_ref is not None:
                o_ref[0, sl, :] = jnp.zeros((ROW_CHUNK, d), F32)

        _for_chunks(lo, hi, apply, 4)

    _segments(i, tm, rows, t_lat, seg)


def _accumulate(o_ref, a, w_ref, ss_ref):
    ss = None
    for c0 in range(0, o_ref.shape[-1], COL_CHUNK):
        new = o_ref[0, :, c0:c0 + COL_CHUNK] + _dot(a, w_ref[:, c0:c0 + COL_CHUNK])
        o_ref[0, :, c0:c0 + COL_CHUNK] = new
        sq = new * new
        for l0 in range(0, COL_CHUNK, LANE):
            ss = sq[:, l0:l0 + LANE] if ss is None else ss + sq[:, l0:l0 + LANE]
    ss_ref[...] = ss


def _epilogue(i, h_ref, o_ref, inv_ref, gpost_ref, modl_ref, modc_ref, gate_idx, factor, tm, rows, t_lat):
    d = h_ref.shape[-1]

    ms = jnp.sum(inv_ref[...], axis=-1, keepdims=True) * (1.0 / d)
    inv_ref[...] = jnp.broadcast_to(lax.rsqrt(ms + EPS), (tm, LANE))
    g = gpost_ref[...] * factor

    def seg(lo, hi, is_ctx):
        gain = g * (modc_ref[gate_idx:gate_idx + 1, :] if is_ctx else modl_ref[0, gate_idx:gate_idx + 1, :])

        def apply(sl):
            o_ref[0, sl, :] = h_ref[0, sl, :] + o_ref[0, sl, :] * _lanes(inv_ref[sl, :], d) * gain

        _for_chunks(lo, hi, apply, 4)

    _segments(i, tm, rows, t_lat, seg)


def _ffn_kernel(h_ref, modl_ref, modc_ref, gpre_ref, gpost_ref, wg_ref, wu_ref, wd_ref,
                o_ref, n_ref, inv_ref, *, tm, rows, t_lat, base, nk):
    i = pl.program_id(1)
    k = pl.program_id(2)

    @pl.when(k == 0)
    def _():
        _prologue(i, h_ref, n_ref, o_ref, inv_ref, gpre_ref, modl_ref, modc_ref, base, tm, rows, t_lat)

    n = n_ref[...]
    g = _dot(n, wg_ref[...])
    u = _dot(n, wu_ref[...])
    a = (g * jax.nn.sigmoid(g) * u).astype(BF16)
    _accumulate(o_ref, a, wd_ref, inv_ref)

    @pl.when(k == nk - 1)
    def _():
        _epilogue(i, h_ref, o_ref, inv_ref, gpost_ref, modl_ref, modc_ref, base + 2, FFN_HALF, tm, rows, t_lat)


def ffn_sublayer(h, modl, modc, g_pre, g_post, wg, wu, wd, *, widx, base, t_lat, rows, tm, tf=512):
    bsz, _, d = h.shape
    dff = wg.shape[-1]
    nk = dff // tf
    wl, ws = widx
    kern = functools.partial(_ffn_kernel, tm=tm, rows=rows, t_lat=t_lat, base=base, nk=nk)
    return pl.pallas_call(
        kern,
        out_shape=jax.ShapeDtypeStruct((bsz, rows, d), F32),
        grid=(bsz, rows // tm, nk),
        in_specs=[
            pl.BlockSpec((1, tm, d), lambda b, i, k: (b, i, 0)),
            pl.BlockSpec((1, N_MOD, d), lambda b, i, k: (b, 0, 0)),
            pl.BlockSpec((N_MOD, d), lambda b, i, k: (0, 0)),
            pl.BlockSpec((1, d), lambda b, i, k: (0, 0)),
            pl.BlockSpec((1, d), lambda b, i, k: (0, 0)),
            pl.BlockSpec((None, None, d, tf), lambda b, i, k: (wl, ws, 0, k)),
            pl.BlockSpec((None, None, d, tf), lambda b, i, k: (wl, ws, 0, k)),
            pl.BlockSpec((None, None, tf, d), lambda b, i, k: (wl, ws, k, 0)),
        ],
        out_specs=pl.BlockSpec((1, tm, d), lambda b, i, k: (b, i, 0)),
        scratch_shapes=[pltpu.VMEM((tm, d), BF16), pltpu.VMEM((tm, LANE), F32)],
        compiler_params=_cparams(("parallel", "parallel", "arbitrary")),
        name="ffn_sublayer",
    )(h, modl, modc, g_pre.reshape(1, d), g_post.reshape(1, d), wg, wu, wd)


def _proj_kernel(h_ref, modl_ref, modc_ref, gpre_ref, w_ref, wdt_ref, o_ref, odt_ref, n_ref, inv_ref,
                 *, tm, rows, t_lat, base):
    i = pl.program_id(1)
    j = pl.program_id(2)

    @pl.when(j == 0)
    def _():
        _prologue(i, h_ref, n_ref, None, inv_ref, gpre_ref, modl_ref, modc_ref, base, tm, rows, t_lat)
        odt_ref[0] = _dot(n_ref[...], wdt_ref[...])

    o_ref[0] = _dot(n_ref[...], w_ref[...])


def in_projection(h, modl, modc, g_pre, w_main, w_dt, *, base, t_lat, tm, tn=512):
    bsz, s, d = h.shape
    ncol = w_main.shape[1]
    ndt = w_dt.shape[1]
    kern = functools.partial(_proj_kernel, tm=tm, rows=s, t_lat=t_lat, base=base)
    return pl.pallas_call(
        kern,
        out_shape=(jax.ShapeDtypeStruct((bsz, s, ncol), F32),
                   jax.ShapeDtypeStruct((bsz, s, ndt), F32)),
        grid=(bsz, s // tm, ncol // tn),
        in_specs=[
            pl.BlockSpec((1, tm, d), lambda b, i, j: (b, i, 0)),
            pl.BlockSpec((1, N_MOD, d), lambda b, i, j: (b, 0, 0)),
            pl.BlockSpec((N_MOD, d), lambda b, i, j: (0, 0)),
            pl.BlockSpec((1, d), lambda b, i, j: (0, 0)),
            pl.BlockSpec((d, tn), lambda b, i, j: (0, j)),
            pl.BlockSpec((d, ndt), lambda b, i, j: (0, 0)),
        ],
        out_specs=(pl.BlockSpec((1, tm, tn), lambda b, i, j: (b, i, j)),
                   pl.BlockSpec((1, tm, ndt), lambda b, i, j: (b, i, 0))),
        scratch_shapes=[pltpu.VMEM((tm, d), BF16), pltpu.VMEM((tm, LANE), F32)],
        compiler_params=_cparams(("parallel", "parallel", "arbitrary")),
        name="in_projection",
    )(h, modl, modc, g_pre.reshape(1, d), w_main, w_dt)


def _merge_kernel(h_ref, modl_ref, modc_ref, gpre_ref, gpost_ref, b0_ref, b1_ref, b2_ref, b3_ref,
                  wg_ref, wb_ref, wo_ref, o_ref, n_ref, inv_ref, *, tm, rows, t_lat, base, nk):
    i = pl.program_id(1)
    k = pl.program_id(2)

    @pl.when(k == 0)
    def _():
        _prologue(i, h_ref, n_ref, o_ref, inv_ref, gpre_ref, modl_ref, modc_ref, base, tm, rows, t_lat)

    n = n_ref[...]
    y = None
    for bi, br_ref in enumerate((b0_ref, b1_ref, b2_ref, b3_ref)):
        t = jax.nn.sigmoid(_dot(n, wg_ref[bi])) * _dot(br_ref[0], wb_ref[bi])
        y = t if y is None else y + t
    _accumulate(o_ref, y.astype(BF16), wo_ref, inv_ref)

    @pl.when(k == nk - 1)
    def _():
        _epilogue(i, h_ref, o_ref, inv_ref, gpost_ref, modl_ref, modc_ref, base + 2, 1.0, tm, rows, t_lat)


def merge_sublayer(h, modl, modc, g_pre, g_post, branches, wg, wb, wo, *, wl, base, t_lat, rows, tm, tn=256):
    bsz, _, d = h.shape
    w = branches[0].shape[-1]
    nk = d // tn
    kern = functools.partial(_merge_kernel, tm=tm, rows=rows, t_lat=t_lat, base=base, nk=nk)
    br_spec = pl.BlockSpec((1, tm, w), lambda b, i, k: (b, i, 0))
    return pl.pallas_call(
        kern,
        out_shape=jax.ShapeDtypeStruct((bsz, rows, d), F32),
        grid=(bsz, rows // tm, nk),
        in_specs=[
            pl.BlockSpec((1, tm, d), lambda b, i, k: (b, i, 0)),
            pl.BlockSpec((1, N_MOD, d), lambda b, i, k: (b, 0, 0)),
            pl.BlockSpec((N_MOD, d), lambda b, i, k: (0, 0)),
            pl.BlockSpec((1, d), lambda b, i, k: (0, 0)),
            pl.BlockSpec((1, d), lambda b, i, k: (0, 0)),
            br_spec, br_spec, br_spec, br_spec,
            pl.BlockSpec((None, 4, d, tn), lambda b, i, k: (wl, 0, 0, k)),
            pl.BlockSpec((None, 4, w, tn), lambda b, i, k: (wl, 0, 0, k)),
            pl.BlockSpec((None, tn, d), lambda b, i, k: (wl, k, 0)),
        ],
        out_specs=pl.BlockSpec((1, tm, d), lambda b, i, k: (b, i, 0)),
        scratch_shapes=[pltpu.VMEM((tm, d), BF16), pltpu.VMEM((tm, LANE), F32)],
        compiler_params=_cparams(("parallel", "parallel", "arbitrary")),
        name="merge_sublayer",
    )(h, modl, modc, g_pre.reshape(1, d), g_post.reshape(1, d), *branches, wg, wb, wo)


def _dwconv_rows(x, w, b, t_lat):
    s = x.shape[0]
    t = lax.broadcasted_iota(jnp.int32, (s, 1), 0)
    tl = jnp.where(t >= t_lat, t - t_lat, t)
    seg_len = jnp.where(t >= t_lat, s - t_lat, t_lat)
    y = x * w[1:2, :] + b
    for j, d in ((0, -1), (2, 1), (3, 2)):
        xs = pltpu.roll(x, (-d) % s, 0)
        ok = (tl + d >= 0) & (tl + d < seg_len)
        y = y + jnp.where(ok, xs, 0.0) * w[j:j + 1, :]
    return y


def _conv_silu_kernel(x_ref, w_ref, b_ref, o_ref, *, t_lat):
    y = _dwconv_rows(x_ref[0], w_ref[...], b_ref[...], t_lat)
    o_ref[0] = y * jax.nn.sigmoid(y)


def ssm_conv(p, conv_w, conv_b, *, col0, t_lat, tc=512):
    bsz, s, _ = p.shape
    c = conv_w.shape[1]
    off = col0 // tc
    return pl.pallas_call(
        functools.partial(_conv_silu_kernel, t_lat=t_lat),
        out_shape=jax.ShapeDtypeStruct((bsz, s, c), F32),
        grid=(bsz, c // tc),
        in_specs=[
            pl.BlockSpec((1, s, tc), lambda b, j: (b, 0, off + j)),
            pl.BlockSpec((CONV_K, tc), lambda b, j: (0, j)),
            pl.BlockSpec((1, tc), lambda b, j: (0, j)),
        ],
        out_specs=pl.BlockSpec((1, s, tc), lambda b, j: (b, 0, j)),
        compiler_params=_cparams(("parallel", "parallel")),
        name="ssm_conv",
    )(p, conv_w, conv_b.reshape(1, c))


def _ssd_direction(xc, dtraw, a, dtb, expand, state_ref, reverse):
    q = xc.shape[0]
    nh = dtraw.shape[1]
    xw = nh * SSM_HEAD_DIM
    gw = xw // SSM_GROUPS
    hg = nh // SSM_GROUPS
    x = xc[:, :xw]
    dt = jax.nn.softplus(dtraw + dtb)
    d = dt * a
    li = lax.broadcasted_iota(jnp.int32, (q, q), 0)
    si = lax.broadcasted_iota(jnp.int32, (q, q), 1)
    mask = (si >= li) if reverse else (si <= li)
    tri = jnp.where(mask, 1.0, 0.0).astype(BF16)
    tri_t = jnp.where((li >= si) if reverse else (li <= si), 1.0, 0.0).astype(BF16)
    hi = d.astype(BF16)
    mid = (d - hi.astype(F32)).astype(BF16)
    lo = (d - hi.astype(F32) - mid.astype(F32)).astype(BF16)
    cs = _dot(tri, hi) + _dot(tri, mid) + _dot(tri, lo)
    cs_t = _dot_tn(hi, tri_t) + _dot_tn(mid, tri_t) + _dot_tn(lo, tri_t)
    last = 0 if reverse else q - 1
    total = cs[last:last + 1, :]
    w_state = jnp.exp(total - cs) * dt
    e_off = jnp.exp(cs)
    ex = _split_dot(jnp.concatenate([w_state, dt, e_off], axis=0), expand)
    x_state = (x * ex[:q]).astype(BF16)
    x_dt = x * ex[q:2 * q]
    e_off_x = ex[2 * q:]
    lane = lax.broadcasted_iota(jnp.int32, (q, LANE), 1)
    ys = []
    for g in range(SSM_GROUPS):
        bm = xc[:, xw + g * SSM_STATE: xw + (g + 1) * SSM_STATE].astype(BF16)
        cm = xc[:, xw + (SSM_GROUPS + g) * SSM_STATE: xw + (SSM_GROUPS + g + 1) * SSM_STATE].astype(BF16)
        cb = _dot_nt(cm, bm)
        st = state_ref[g]
        y_off = _dot(cm, st.astype(BF16)) * e_off_x[:, g * gw:(g + 1) * gw]
        yd = []
        for hp in range(hg // 2):
            ms = []
            for hh in range(2):
                h = g * hg + hp * 2 + hh
                seg = cs[:, h:h + 1] - cs_t[h:h + 1, :]
                ms.append((cb * jnp.where(mask, jnp.exp(seg), 0.0)).astype(BF16))
            c0 = g * gw + hp * LANE
            xp = x_dt[:, c0:c0 + LANE]
            rhs = jnp.concatenate([jnp.where(lane < SSM_HEAD_DIM, xp, 0.0).astype(BF16),
                                   jnp.where(lane >= SSM_HEAD_DIM, xp, 0.0).astype(BF16)], axis=0)
            yd.append(_dot(jnp.concatenate(ms, axis=1), rhs))
        ys.append(jnp.concatenate(yd, axis=1) + y_off)
        decay = e_off_x[last:last + 1, g * gw:(g + 1) * gw]
        state_ref[g] = decay * st + _dot_tn(bm, x_state[:, g * gw:(g + 1) * gw])
    return jnp.concatenate(ys, axis=1)


def _ssd_kernel(xf_ref, xb_ref, dtf_ref, dtb_ref, a_ref, bias_ref, exp_ref, yf_ref, yb_ref,
                sf_ref, sb_ref, *, nh):
    @pl.when(pl.program_id(1) == 0)
    def _():
        sf_ref[...] = jnp.zeros_like(sf_ref)
        sb_ref[...] = jnp.zeros_like(sb_ref)

    ex = exp_ref[...]
    yf_ref[0] = _ssd_direction(xf_ref[0], dtf_ref[0][:, :nh], a_ref[0:1, :], bias_ref[0:1, :],
                               ex, sf_ref, False)
    yb_ref[0] = _ssd_direction(xb_ref[0], dtb_ref[0][:, nh:2 * nh], a_ref[1:2, :], bias_ref[1:2, :],
                               ex, sb_ref, True)


def ssd_scan(xc, dt_raw, a_log, dt_bias, *, t_lat):
    bsz, s, cw = xc.shape
    nh = a_log.shape[1]
    xw = nh * SSM_HEAD_DIM
    q = SSM_CHUNK
    nt = s // q
    nl = t_lat // q
    nc = nt - nl
    a = -jnp.exp(a_log.astype(F32))
    expand = (jnp.arange(xw)[None, :] // SSM_HEAD_DIM == jnp.arange(nh)[:, None]).astype(BF16)

    def fidx(j):
        return jnp.where(j < nc, nl + j, j - nc)

    def bidx(j):
        return nt - 1 - j

    ndt = dt_raw.shape[-1]
    return pl.pallas_call(
        functools.partial(_ssd_kernel, nh=nh),
        out_shape=(jax.ShapeDtypeStruct((bsz, s, xw), F32), jax.ShapeDtypeStruct((bsz, s, xw), F32)),
        grid=(bsz, nt),
        in_specs=[
            pl.BlockSpec((1, q, cw), lambda b, j: (b, fidx(j), 0)),
            pl.BlockSpec((1, q, cw), lambda b, j: (b, bidx(j), 0)),
            pl.BlockSpec((1, q, ndt), lambda b, j: (b, fidx(j), 0)),
            pl.BlockSpec((1, q, ndt), lambda b, j: (b, bidx(j), 0)),
            pl.BlockSpec((2, nh), lambda b, j: (0, 0)),
            pl.BlockSpec((2, nh), lambda b, j: (0, 0)),
            pl.BlockSpec((nh, xw), lambda b, j: (0, 0)),
        ],
        out_specs=(pl.BlockSpec((1, q, xw), lambda b, j: (b, fidx(j), 0)),
                   pl.BlockSpec((1, q, xw), lambda b, j: (b, bidx(j), 0))),
        scratch_shapes=[pltpu.VMEM((SSM_GROUPS, SSM_STATE, xw // SSM_GROUPS), F32),
                        pltpu.VMEM((SSM_GROUPS, SSM_STATE, xw // SSM_GROUPS), F32)],
        compiler_params=_cparams(("parallel", "arbitrary")),
        name="ssd_scan",
    )(xc, xc, dt_raw, dt_raw, a, dt_bias.astype(F32), expand)


def _ssm_finish_kernel(yf_ref, yb_ref, xs_ref, z_ref, dsk_ref, g_ref, o_ref):
    z = z_ref[0]
    y = (yf_ref[0] + yb_ref[0] + dsk_ref[...] * xs_ref[0]) * (z * jax.nn.sigmoid(z))
    gw = y.shape[1] // SSM_GROUPS
    outs = []
    for g in range(SSM_GROUPS):
        outs.append(_rms(y[:, g * gw:(g + 1) * gw], g_ref[:, g * gw:(g + 1) * gw]))
    o_ref[0] = jnp.concatenate(outs, axis=1).astype(o_ref.dtype)


def ssm_finish(y_f, y_b, xc, p, d_skip, g_norm, *, z_col0, tm):
    bsz, s, xw = y_f.shape
    dsk = jnp.repeat(d_skip.astype(F32), SSM_HEAD_DIM).reshape(1, xw)
    zoff = z_col0 // xw
    spec = pl.BlockSpec((1, tm, xw), lambda b, i: (b, i, 0))
    return pl.pallas_call(
        _ssm_finish_kernel,
        out_shape=jax.ShapeDtypeStruct((bsz, s, xw), BF16),
        grid=(bsz, s // tm),
        in_specs=[spec, spec, spec,
                  pl.BlockSpec((1, tm, xw), lambda b, i: (b, i, zoff)),
                  pl.BlockSpec((1, xw), lambda b, i: (0, 0)),
                  pl.BlockSpec((1, xw), lambda b, i: (0, 0))],
        out_specs=spec,
        compiler_params=_cparams(("parallel", "parallel")),
        name="ssm_finish",
    )(y_f, y_b, xc, p, dsk, g_norm.reshape(1, xw))


def _rope_tables(t_lat, head_dim, reps):
    rows = t_lat // GRID_W
    row = jnp.repeat(jnp.arange(rows, dtype=jnp.int32), GRID_W).astype(F32)
    col = (jnp.arange(t_lat, dtype=jnp.int32) % GRID_W).astype(F32)
    half = head_dim // 2
    inv = ROPE_THETA ** (-(jnp.arange(0, half, 2, dtype=F32) / half))
    ang_r = row[:, None] * inv[None, :]
    ang_c = col[:, None] * inv[None, :]
    cos = jnp.concatenate([jnp.cos(ang_r), jnp.cos(ang_r), jnp.cos(ang_c), jnp.cos(ang_c)], axis=1)
    sin = jnp.concatenate([-jnp.sin(ang_r), jnp.sin(ang_r), -jnp.sin(ang_c), jnp.sin(ang_c)], axis=1)
    return jnp.tile(cos, (1, reps)), jnp.tile(sin, (1, reps))


def _rope(v, cos, sin, quarter):
    w = v.shape[1]
    lane = lax.broadcasted_iota(jnp.int32, v.shape, 1)
    first = (lane % (2 * quarter)) < quarter
    swapped = jnp.where(first, pltpu.roll(v, w - quarter, 1), pltpu.roll(v, quarter, 1))
    return v * cos + swapped * sin


LOG2E = math.log2(math.e)


def _softmax_parts(s2):
    m = jnp.max(s2, axis=-1, keepdims=True)
    e = jnp.exp2(s2 - m)
    return e, jnp.sum(e, axis=-1, keepdims=True)


def _gqa_kernel(q_ref, k_ref, v_ref, cos_ref, sin_ref, cosq_ref, sinq_ref, gq_ref, gk_ref, o_ref,
                kp_ref, vp_ref, *, t_lat, nq_lat, kps):
    qi = pl.program_id(2)
    d = GQA_HEAD_DIM
    quarter = d // 4

    @pl.when(qi == 0)
    def _():
        for h in range(kps):
            cs = slice(h * d, (h + 1) * d)
            kn = _rms(k_ref[0, :, cs], gk_ref[...])
            kp_ref[:t_lat, cs] = _rope(kn[:t_lat], cos_ref[...], sin_ref[...], quarter).astype(BF16)
            kp_ref[t_lat:, cs] = kn[t_lat:].astype(BF16)
            vp_ref[:, cs] = v_ref[0, :, cs].astype(BF16)

    scale = d ** -0.5 * LOG2E

    def attend(rope_q, k0):
        for j in range(kps * GQA_GROUP):
            h = j // GQA_GROUP
            qh = _rms(q_ref[0, :, j * d:(j + 1) * d], gq_ref[...])
            if rope_q:
                qh = _rope(qh, cosq_ref[...], sinq_ref[...], quarter)
            e, l = _softmax_parts(_dot_nt((qh * scale).astype(BF16), kp_ref[k0:, h * d:(h + 1) * d]))
            o = _dot(e.astype(BF16), vp_ref[k0:, h * d:(h + 1) * d]) / l
            o_ref[0, :, j * d:(j + 1) * d] = o.astype(o_ref.dtype)

    @pl.when(qi < nq_lat)
    def _():
        attend(True, 0)

    @pl.when(qi >= nq_lat)
    def _():
        attend(False, t_lat)


def gqa_attention(p, g_q, g_k, *, q_col0, k_col0, v_col0, n_kv, t_lat, with_ctx, tq=256, kps=2):
    bsz, s, _ = p.shape
    d = GQA_HEAD_DIM
    gw = kps * GQA_GROUP * d
    kw = kps * d
    nq_lat = t_lat // tq
    nq = (s if with_ctx else t_lat) // tq
    cos, sin = _rope_tables(t_lat, d, 1)
    qo, ko, vo = q_col0 // gw, k_col0 // kw, v_col0 // kw
    kern = functools.partial(_gqa_kernel, t_lat=t_lat, nq_lat=nq_lat, kps=kps)
    qtab = lambda b, h, i: (jnp.minimum(i, nq_lat - 1), 0)
    return pl.pallas_call(
        kern,
        out_shape=jax.ShapeDtypeStruct((bsz, nq * tq, n_kv * GQA_GROUP * d), BF16),
        grid=(bsz, n_kv // kps, nq),
        in_specs=[
            pl.BlockSpec((1, tq, gw), lambda b, h, i: (b, i, qo + h)),
            pl.BlockSpec((1, s, kw), lambda b, h, i: (b, 0, ko + h)),
            pl.BlockSpec((1, s, kw), lambda b, h, i: (b, 0, vo + h)),
            pl.BlockSpec((t_lat, d), lambda b, h, i: (0, 0)),
            pl.BlockSpec((t_lat, d), lambda b, h, i: (0, 0)),
            pl.BlockSpec((tq, d), qtab),
            pl.BlockSpec((tq, d), qtab),
            pl.BlockSpec((1, d), lambda b, h, i: (0, 0)),
            pl.BlockSpec((1, d), lambda b, h, i: (0, 0)),
        ],
        out_specs=pl.BlockSpec((1, tq, gw), lambda b, h, i: (b, i, h)),
        scratch_shapes=[pltpu.VMEM((s, kw), BF16), pltpu.VMEM((s, kw), BF16)],
        compiler_params=_cparams(("parallel", "parallel", "arbitrary")),
        name="gqa_attention",
    )(p, p, p, cos, sin, cos, sin, g_q.reshape(1, d), g_k.reshape(1, d))


def _diff_kernel(q_ref, k_ref, v_ref, cos_ref, sin_ref, cosq_ref, sinq_ref, lq_ref, lk_ref, g_ref,
                 o_ref, kp_ref, vp_ref, *, t_lat, nq_lat, lambda_init, hps):
    qi = pl.program_id(2)
    d = DIFF_HEAD_DIM
    hw = 2 * d
    quarter = d // 4

    @pl.when(qi == 0)
    def _():
        for j in range(hps):
            cs = slice(j * hw, (j + 1) * hw)
            kp_ref[:t_lat, cs] = _rope(k_ref[0, :t_lat, cs], cos_ref[...], sin_ref[...], quarter).astype(BF16)
            kp_ref[t_lat:, cs] = k_ref[0, t_lat:, cs].astype(BF16)
            vp_ref[:, cs] = v_ref[0, :, cs].astype(BF16)

    prod = lq_ref[...] * lk_ref[...]
    sums = jnp.sum(prod, axis=-1, keepdims=True)
    lam = jnp.exp(sums[0:1]) - jnp.exp(sums[1:2]) + lambda_init
    scale = d ** -0.5 * LOG2E

    def attend(rope_q, k0):
        for j in range(hps):
            cs = slice(j * hw, (j + 1) * hw)
            q = q_ref[0, :, cs]
            if rope_q:
                q = _rope(q, cosq_ref[...], sinq_ref[...], quarter)
            q = q * scale
            lane = lax.broadcasted_iota(jnp.int32, q.shape, 1)
            kk = kp_ref[k0:, cs]
            q0 = jnp.where(lane < d, q, 0.0).astype(BF16)
            q1 = jnp.where(lane >= d, q, 0.0).astype(BF16)
            e0, l0 = _softmax_parts(_dot_nt(q0, kk))
            e1, l1 = _softmax_parts(_dot_nt(q1, kk))
            pd = e0 - (lam * l0 / l1) * e1
            o = _dot(pd.astype(BF16), vp_ref[k0:, cs]) / l0
            o_ref[0, :, cs] = (_rms(o, g_ref[...]) * (1.0 - lambda_init)).astype(o_ref.dtype)

    @pl.when(qi < nq_lat)
    def _():
        attend(True, 0)

    @pl.when(qi >= nq_lat)
    def _():
        attend(False, t_lat)


def diff_attention(p, lambda_q, lambda_k, g_norm, *, q_col0, k_col0, v_col0, n_heads, t_lat,
                   with_ctx, lambda_init, tq=256, hps=4):
    bsz, s, _ = p.shape
    hw = 2 * DIFF_HEAD_DIM
    gw = hps * hw
    nq_lat = t_lat // tq
    nq = (s if with_ctx else t_lat) // tq
    cos, sin = _rope_tables(t_lat, DIFF_HEAD_DIM, 2)
    qo, ko, vo = q_col0 // gw, k_col0 // gw, v_col0 // gw
    kern = functools.partial(_diff_kernel, t_lat=t_lat, nq_lat=nq_lat, lambda_init=lambda_init, hps=hps)
    qtab = lambda b, h, i: (jnp.minimum(i, nq_lat - 1), 0)
    return pl.pallas_call(
        kern,
        out_shape=jax.ShapeDtypeStruct((bsz, nq * tq, n_heads * hw), BF16),
        grid=(bsz, n_heads // hps, nq),
        in_specs=[
            pl.BlockSpec((1, tq, gw), lambda b, h, i: (b, i, qo + h)),
            pl.BlockSpec((1, s, gw), lambda b, h, i: (b, 0, ko + h)),
            pl.BlockSpec((1, s, gw), lambda b, h, i: (b, 0, vo + h)),
            pl.BlockSpec((t_lat, hw), lambda b, h, i: (0, 0)),
            pl.BlockSpec((t_lat, hw), lambda b, h, i: (0, 0)),
            pl.BlockSpec((tq, hw), qtab),
            pl.BlockSpec((tq, hw), qtab),
            pl.BlockSpec((2, DIFF_HEAD_DIM), lambda b, h, i: (0, 0)),
            pl.BlockSpec((2, DIFF_HEAD_DIM), lambda b, h, i: (0, 0)),
            pl.BlockSpec((1, hw), lambda b, h, i: (0, 0)),
        ],
        out_specs=pl.BlockSpec((1, tq, gw), lambda b, h, i: (b, i, h)),
        scratch_shapes=[pltpu.VMEM((s, gw), BF16), pltpu.VMEM((s, gw), BF16)],
        compiler_params=_cparams(("parallel", "parallel", "arbitrary")),
        name="diff_attention",
    )(p, p, p, cos, sin, cos, sin, lambda_q.astype(F32), lambda_k.astype(F32), g_norm.reshape(1, hw))


def _scan_rows(a_ref, b_ref, h_ref, row0, nblk, carry, reverse, accumulate):
    rows = lax.broadcasted_iota(jnp.int32, (SUBLANE, a_ref.shape[1]), 0)

    def body(i, carry):
        blk = (nblk - 1 - i) if reverse else i
        r0 = pl.multiple_of(row0 + blk * SUBLANE, SUBLANE)
        a = a_ref[pl.ds(r0, SUBLANE), :]
        b = b_ref[pl.ds(r0, SUBLANE), :]
        for sft in (1, 2, 4):
            if reverse:
                a_s = pltpu.roll(a, SUBLANE - sft, 0)
                b_s = pltpu.roll(b, SUBLANE - sft, 0)
                ok = rows < SUBLANE - sft
            else:
                a_s = pltpu.roll(a, sft, 0)
                b_s = pltpu.roll(b, sft, 0)
                ok = rows >= sft
            b = jnp.where(ok, a * b_s + b, b)
            a = jnp.where(ok, a * a_s, a)
        h = b + a * carry
        if accumulate:
            h_ref[pl.ds(r0, SUBLANE), :] += h
        else:
            h_ref[pl.ds(r0, SUBLANE), :] = h
        return h[0:1, :] if reverse else h[SUBLANE - 1:SUBLANE, :]

    return lax.fori_loop(0, nblk, body, carry, unroll=4)


LRU_ROW_CHUNK = 256


def _sigmoid_tanh(x):
    return 0.5 * jnp.tanh(0.5 * x) + 0.5


def _lru_kernel(x_ref, g_ref, cw_ref, cb_ref, wrg_ref, wig_ref, brg_ref, big_ref, lam_ref, o_ref,
                xr_ref, a_ref, b_ref, h_ref, *, t_lat):
    s, w = x_ref.shape[1], x_ref.shape[2]
    xr_ref[...] = _dwconv_rows(x_ref[0], cw_ref[...], cb_ref[...], t_lat)
    n_lat = t_lat // SUBLANE
    n_ctx = (s - t_lat) // SUBLANE
    zero = jnp.zeros((1, w), F32)
    for dr in range(2):
        sp = jax.nn.softplus(-lam_ref[dr:dr + 1, :])

        def gates(r, carry, dr=dr, sp=sp):
            rows = pl.ds(pl.multiple_of(r * LRU_ROW_CHUNK, LRU_ROW_CHUNK), LRU_ROW_CHUNK)
            for c in range(w // LANE):
                cs = slice(c * LANE, (c + 1) * LANE)
                xr = xr_ref[rows, cs]
                xb = xr.astype(BF16)
                rg = _sigmoid_tanh(_dot(xb, wrg_ref[dr, c]) + brg_ref[dr:dr + 1, cs])
                ig = _sigmoid_tanh(_dot(xb, wig_ref[dr, c]) + big_ref[dr:dr + 1, cs])
                log_a = -LRU_C * rg * sp[:, cs]
                a = jnp.exp(log_a)
                a_ref[rows, cs] = a
                b_ref[rows, cs] = jnp.sqrt(-jnp.tanh(log_a) * (a * a + 1.0)) * (ig * xr)
            return carry

        lax.fori_loop(0, s // LRU_ROW_CHUNK, gates, 0)
        if dr == 0:
            c = _scan_rows(a_ref, b_ref, h_ref, t_lat, n_ctx, zero, False, False)
            _scan_rows(a_ref, b_ref, h_ref, 0, n_lat, c, False, False)
        else:
            c = _scan_rows(a_ref, b_ref, h_ref, t_lat, n_ctx, zero, True, True)
            _scan_rows(a_ref, b_ref, h_ref, 0, n_lat, c, True, True)

    def finish(r, carry):
        rows = pl.ds(pl.multiple_of(r * LRU_ROW_CHUNK, LRU_ROW_CHUNK), LRU_ROW_CHUNK)
        o_ref[0, rows, :] = (jax.nn.gelu(g_ref[0, rows, :]) * h_ref[rows, :]).astype(o_ref.dtype)
        return carry

    lax.fori_loop(0, s // LRU_ROW_CHUNK, finish, 0)


def rglru(p, conv_w, conv_b, w_rg, b_rg, w_ig, b_ig, lam, *, g_col0, x_col0, t_lat, cw=512):
    bsz, s, _ = p.shape
    width = conv_w.shape[1]
    ncg = width // cw
    nsub = cw // LANE
    go, xo = g_col0 // cw, x_col0 // cw

    def pair_blocks(w):
        nb = w.shape[1]
        w = w.reshape(2, nb // 2, 2, LRU_BLOCK_DIM, LRU_BLOCK_DIM)
        z = jnp.zeros_like(w[:, :, 0])
        top = jnp.concatenate([w[:, :, 0], z], axis=-1)
        bot = jnp.concatenate([z, w[:, :, 1]], axis=-1)
        return jnp.concatenate([top, bot], axis=-2).astype(BF16)

    wspec = pl.BlockSpec((2, nsub, LANE, LANE), lambda b, c: (0, c, 0, 0))
    vspec = pl.BlockSpec((2, cw), lambda b, c: (0, c))
    return pl.pallas_call(
        functools.partial(_lru_kernel, t_lat=t_lat),
        out_shape=jax.ShapeDtypeStruct((bsz, s, width), BF16),
        grid=(bsz, ncg),
        in_specs=[
            pl.BlockSpec((1, s, cw), lambda b, c: (b, 0, xo + c)),
            pl.BlockSpec((1, s, cw), lambda b, c: (b, 0, go + c)),
            pl.BlockSpec((CONV_K, cw), lambda b, c: (0, c)),
            pl.BlockSpec((1, cw), lambda b, c: (0, c)),
            wspec, wspec, vspec, vspec, vspec,
        ],
        out_specs=pl.BlockSpec((1, s, cw), lambda b, c: (b, 0, c)),
        scratch_shapes=[pltpu.VMEM((s, cw), F32)] * 4,
        compiler_params=_cparams(("parallel", "parallel")),
        name="rglru",
    )(p, p, conv_w.astype(F32), conv_b.reshape(1, width).astype(F32), pair_blocks(w_rg), pair_blocks(w_ig),
      b_rg.astype(F32), b_ig.astype(F32), lam.astype(F32))


def _split_w_in(w_in, mix_w, n_ssm_heads):
    gn = SSM_GROUPS * SSM_STATE
    kv = mix_w // GQA_GROUP
    sizes = (mix_w, mix_w + 2 * gn, 2 * n_ssm_heads, mix_w, mix_w, mix_w, mix_w, kv, kv, mix_w, mix_w)
    names = ("z", "xbc", "dt", "q_d", "k_d", "v_d", "q_g", "k_g", "v_g", "g_lru", "x_lru")
    parts = {}
    o = 0
    for nme, sz in zip(names, sizes):
        parts[nme] = w_in[:, o:o + sz]
        o += sz
    order = ("q_g", "q_d", "k_d", "v_d", "z", "g_lru", "x_lru", "xbc", "k_g", "v_g")
    offs = {}
    o = 0
    for nme in order:
        offs[nme] = o
        o += parts[nme].shape[1]
    w_main = jnp.concatenate([parts[nme] for nme in order], axis=1).astype(BF16)
    w_dt = jnp.pad(parts["dt"], ((0, 0), (0, LANE - 2 * n_ssm_heads))).astype(BF16)
    return w_main, w_dt, offs


def kernel(x, c, ctx, c_ctx, w_ada, b_ada, g_pre, g_post, w_ffn_gate, w_ffn_up, w_ffn_down, w_in, conv_w_ssm, conv_b_ssm, a_log, dt_bias, d_skip, g_ssm_norm, lambda_q, lambda_k, g_diff_norm, g_q_norm, g_k_norm, conv_w_lru, conv_b_lru, w_rg, b_rg, w_ig, b_ig, lru_lambda, w_branch, w_gate, w_out):
    bsz, t_lat, d = x.shape
    depth = w_ada.shape[0]
    mix_w = d // 2
    n_ssm_heads = a_log.shape[-1]
    s = t_lat + ctx.shape[1]
    tiles = {"ffn": s // 3, "proj": s // 2, "merge": s // 4, "finish": s // 4, "lat": t_lat // 4}
    assert all(t % ROW_CHUNK == 0 for t in tiles.values()) and t_lat % ROW_CHUNK == 0, tiles

    r = ((bsz + 1 + 15) // 16) * 16
    cc = jnp.zeros((r, d), F32).at[:bsz].set(c).at[bsz].set(c_ctx)
    mods = adaln(cc, w_ada, b_ada)

    wg_all, wu_all, wd_all = (w.astype(BF16) for w in (w_ffn_gate, w_ffn_up, w_ffn_down))
    wgate_all, wbranch_all, wout_all = (w.astype(BF16) for w in (w_gate, w_branch, w_out))

    h = jnp.concatenate([x, ctx], axis=1)
    for l in range(depth):
        last = l == depth - 1
        lambda_init = 0.8 - 0.6 * math.exp(-0.3 * l)
        modl = mods[l, :bsz].reshape(bsz, N_MOD, d)
        modc = mods[l, bsz].reshape(N_MOD, d)
        w_main, w_dt, offs = _split_w_in(w_in[l], mix_w, n_ssm_heads)

        h = ffn_sublayer(h, modl, modc, g_pre[l, 0], g_post[l, 0], wg_all, wu_all, wd_all,
                         widx=(l, 0), base=0, t_lat=t_lat, rows=s, tm=tiles["ffn"])

        p, dt_raw = in_projection(h, modl, modc, g_pre[l, 1], w_main, w_dt, base=3, t_lat=t_lat,
                                  tm=tiles["proj"], tn=1024)

        xc = ssm_conv(p, conv_w_ssm[l].astype(F32), conv_b_ssm[l].astype(F32), col0=offs["xbc"], t_lat=t_lat)
        y_f, y_b = ssd_scan(xc, dt_raw, a_log[l], dt_bias[l], t_lat=t_lat)
        ssm = ssm_finish(y_f, y_b, xc, p, d_skip[l], g_ssm_norm[l], z_col0=offs["z"], tm=tiles["finish"])

        dif = diff_attention(p, lambda_q[l], lambda_k[l], g_diff_norm[l], q_col0=offs["q_d"],
                             k_col0=offs["k_d"], v_col0=offs["v_d"], n_heads=mix_w // (2 * DIFF_HEAD_DIM),
                             t_lat=t_lat, with_ctx=not last, lambda_init=lambda_init)
        gqa = gqa_attention(p, g_q_norm[l], g_k_norm[l], q_col0=offs["q_g"], k_col0=offs["k_g"],
                            v_col0=offs["v_g"], n_kv=mix_w // (GQA_GROUP * GQA_HEAD_DIM), t_lat=t_lat,
                            with_ctx=not last)
        lru = rglru(p, conv_w_lru[l], conv_b_lru[l], w_rg[l], b_rg[l], w_ig[l], b_ig[l], lru_lambda[l],
                    g_col0=offs["g_lru"], x_col0=offs["x_lru"], t_lat=t_lat)

        rows = t_lat if last else s
        h = merge_sublayer(h, modl, modc, g_pre[l, 1], g_post[l, 1], (ssm, dif, gqa, lru),
                           wgate_all, wbranch_all, wout_all, wl=l,
                           base=3, t_lat=t_lat, rows=rows, tm=tiles["lat" if last else "merge"])
        h = ffn_sublayer(h, modl, modc, g_pre[l, 2], g_post[l, 2], wg_all, wu_all, wd_all,
                         widx=(l, 1), base=6, t_lat=t_lat, rows=rows, tm=tiles["lat" if last else "ffn"])
    return h
```

```python
import functools
import math

import jax
import jax.numpy as jnp
from jax import lax
from jax.experimental import pallas as pl
from jax.experimental.pallas import tpu as pltpu

F32 = jnp.float32
BF16 = jnp.bfloat16

GRID_W = 64
N_MOD = 9
FFN_HALF = 0.5
EPS = 1e-6
ROPE_THETA = 10000.0
CONV_K = 4
SSM_HEAD_DIM = 64
SSM_GROUPS = 2
SSM_STATE = 128
SSM_CHUNK = 128
DIFF_HEAD_DIM = 64
GQA_HEAD_DIM = 128
GQA_GROUP = 4
LRU_BLOCK_DIM = 64
LRU_C = 8.0

LANE = 128
SUBLANE = 8
V7X_VMEM_BYTES = 64 * 1024 * 1024
VMEM_LIMIT = V7X_VMEM_BYTES - 6 * 1024 * 1024


def _cparams(sem):
    return pltpu.CompilerParams(dimension_semantics=sem, vmem_limit_bytes=VMEM_LIMIT)


def _dot(a, b):
    return jnp.dot(a, b, preferred_element_type=F32)


def _dot_nt(a, b):
    return lax.dot_general(a, b, (((1,), (1,)), ((), ())), preferred_element_type=F32)


def _dot_tn(a, b):
    return lax.dot_general(a, b, (((0,), (0,)), ((), ())), preferred_element_type=F32)


def _rms(x, g):
    return x * lax.rsqrt(jnp.mean(x * x, axis=-1, keepdims=True) + EPS) * g


def _split_dot(a_f32, b_bf16):
    hi = a_f32.astype(BF16)
    lo = (a_f32 - hi.astype(F32)).astype(BF16)
    return _dot(hi, b_bf16) + _dot(lo, b_bf16)


def _adaln_kernel(c_ref, w_ref, b_ref, o_ref):
    x = c_ref[...]
    x = (x * jax.nn.sigmoid(x)).astype(BF16)
    o_ref[0] = _dot(x, w_ref[0].astype(BF16)) + b_ref[0]


def adaln(cc, w_ada, b_ada, *, tn=1024):
    depth, d, n = w_ada.shape
    r = cc.shape[0]
    return pl.pallas_call(
        _adaln_kernel,
        out_shape=jax.ShapeDtypeStruct((depth, r, n), F32),
        grid=(depth, n // tn),
        in_specs=[
            pl.BlockSpec((r, d), lambda l, j: (0, 0)),
            pl.BlockSpec((1, d, tn), lambda l, j: (l, 0, j)),
            pl.BlockSpec((1, 1, tn), lambda l, j: (l, 0, j)),
        ],
        out_specs=pl.BlockSpec((1, r, tn), lambda l, j: (l, 0, j)),
        compiler_params=_cparams(("parallel", "parallel")),
        name="adaln",
    )(cc, w_ada, b_ada.reshape(depth, 1, n))


ROW_CHUNK = 16
COL_CHUNK = 512


def _for_chunks(lo, hi, body, unroll):
    if hi <= lo:
        return

    def step(r, carry):
        body(pl.ds(pl.multiple_of(r * ROW_CHUNK, ROW_CHUNK), ROW_CHUNK))
        return carry

    lax.fori_loop(lo, hi, step, 0, unroll=min(unroll, hi - lo))


def _segments(i, tm, rows, t_lat, fn):
    n = tm // ROW_CHUNK
    if rows <= t_lat:
        fn(0, n, False)
        return
    last = rows // tm - 1
    assert last * tm <= t_lat, (tm, rows, t_lat)
    nb = (t_lat - last * tm) // ROW_CHUNK

    @pl.when(i < last)
    def _():
        fn(0, n, False)

    @pl.when(i == last)
    def _():
        fn(0, nb, False)
        fn(nb, n, True)


class _Rows:
    def __init__(self, h_ref, c_ref=None, ctx_row0=0):
        self.h_ref, self.c_ref, self.ctx_row0 = h_ref, c_ref, ctx_row0
        self.width = h_ref.shape[-1]

    def read(self, sl, is_ctx):
        if is_ctx and self.c_ref is not None:
            start = pl.multiple_of(sl.start - self.ctx_row0, ROW_CHUNK)
            return self.c_ref[0, pl.ds(start, ROW_CHUNK), :]
        return self.h_ref[0, sl, :]


def _lanes(inv_tile, width):
    return jnp.concatenate([inv_tile] * (width // LANE), axis=1)


def _prologue(i, src, n_ref, o_ref, inv_ref, gpre_ref, modl_ref, modc_ref, base, tm, rows, t_lat):
    d = src.width
    g = gpre_ref[...]

    def seg_stats(lo, hi, is_ctx):
        def stats(sl):
            x = src.read(sl, is_ctx)
            inv = lax.rsqrt(jnp.mean(x * x, axis=-1, keepdims=True) + EPS)
            inv_ref[sl, :] = jnp.broadcast_to(inv, (ROW_CHUNK, LANE))

        _for_chunks(lo, hi, stats, 16)

    def seg_apply(lo, hi, is_ctx):
        mod = (lambda j: modc_ref[j:j + 1, :]) if is_ctx else (lambda j: modl_ref[0, j:j + 1, :])
        shift = mod(base)
        gain = g * (1.0 + mod(base + 1))

        def apply(sl):
            n_ref[sl, :] = (src.read(sl, is_ctx) * _lanes(inv_ref[sl, :], d) * gain + shift).astype(BF16)
            if o_ref is not None:
                o_ref[0, sl, :] = jnp.zeros((ROW_CHUNK, d), F32)

        _for_chunks(lo, hi, apply, 4)

    _segments(i, tm, rows, t_lat, seg_stats)
    _segments(i, tm, rows, t_lat, seg_apply)


def _accumulate(o_ref, a, w_ref, ss_ref):
    ss = None
    for c0 in range(0, o_ref.shape[-1], COL_CHUNK):
        new = o_ref[0, :, c0:c0 + COL_CHUNK] + _dot(a, w_ref[:, c0:c0 + COL_CHUNK])
        o_ref[0, :, c0:c0 + COL_CHUNK] = new
        sq = new * new
        for l0 in range(0, COL_CHUNK, LANE):
            ss = sq[:, l0:l0 + LANE] if ss is None else ss + sq[:, l0:l0 + LANE]
    ss_ref[...] = ss


def _epilogue(i, src, o_ref, inv_ref, gpost_ref, modl_ref, modc_ref, gate_idx, factor, tm, rows, t_lat):
    d = src.width

    ms = jnp.sum(inv_ref[...], axis=-1, keepdims=True) * (1.0 / d)
    inv_ref[...] = jnp.broadcast_to(lax.rsqrt(ms + EPS), (tm, LANE))
    g = gpost_ref[...] * factor

    def seg(lo, hi, is_ctx):
        gain = g * (modc_ref[gate_idx:gate_idx + 1, :] if is_ctx else modl_ref[0, gate_idx:gate_idx + 1, :])

        def apply(sl):
            o_ref[0, sl, :] = src.read(sl, is_ctx) + o_ref[0, sl, :] * _lanes(inv_ref[sl, :], d) * gain

        _for_chunks(lo, hi, apply, 4)

    _segments(i, tm, rows, t_lat, seg)


def _ffn_kernel(h_ref, c_ref, modl_ref, modc_ref, gpre_ref, gpost_ref, wg_ref, wu_ref, wd_ref,
                o_ref, n_ref, inv_ref, *, tm, rows, t_lat, base, nk, split):
    i = pl.program_id(1)
    k = pl.program_id(2)
    src = _Rows(h_ref, c_ref, t_lat - (rows // tm - 1) * tm) if split else _Rows(h_ref)

    @pl.when(k == 0)
    def _():
        _prologue(i, src, n_ref, o_ref, inv_ref, gpre_ref, modl_ref, modc_ref, base, tm, rows, t_lat)

    n = n_ref[...]
    g = _dot(n, wg_ref[...])
    u = _dot(n, wu_ref[...])
    a = (g * jax.nn.sigmoid(g) * u).astype(BF16)
    _accumulate(o_ref, a, wd_ref, inv_ref)

    @pl.when(k == nk - 1)
    def _():
        _epilogue(i, src, o_ref, inv_ref, gpost_ref, modl_ref, modc_ref, base + 2, FFN_HALF, tm, rows, t_lat)


def ffn_sublayer(h, ctx, modl, modc, g_pre, g_post, wg, wu, wd, *, widx, base, t_lat, rows, tm, tf=512):
    bsz, h_rows, d = h.shape
    split = h_rows < rows
    dff = wg.shape[-1]
    nk = dff // tf
    wl, ws = widx
    kern = functools.partial(_ffn_kernel, tm=tm, rows=rows, t_lat=t_lat, base=base, nk=nk, split=split)
    return pl.pallas_call(
        kern,
        out_shape=jax.ShapeDtypeStruct((bsz, rows, d), F32),
        grid=(bsz, rows // tm, nk),
        in_specs=[
            pl.BlockSpec((1, tm, d), lambda b, i, k: (b, i, 0)),
            pl.BlockSpec((1, ctx.shape[1], d), lambda b, i, k: (b, 0, 0)),
            pl.BlockSpec((1, N_MOD, d), lambda b, i, k: (b, 0, 0)),
            pl.BlockSpec((N_MOD, d), lambda b, i, k: (0, 0)),
            pl.BlockSpec((1, d), lambda b, i, k: (0, 0)),
            pl.BlockSpec((1, d), lambda b, i, k: (0, 0)),
            pl.BlockSpec((None, None, d, tf), lambda b, i, k: (wl, ws, 0, k)),
            pl.BlockSpec((None, None, d, tf), lambda b, i, k: (wl, ws, 0, k)),
            pl.BlockSpec((None, None, tf, d), lambda b, i, k: (wl, ws, k, 0)),
        ],
        out_specs=pl.BlockSpec((1, tm, d), lambda b, i, k: (b, i, 0)),
        scratch_shapes=[pltpu.VMEM((tm, d), BF16), pltpu.VMEM((tm, LANE), F32)],
        compiler_params=_cparams(("parallel", "parallel", "arbitrary")),
        name="ffn_sublayer",
    )(h, ctx, modl, modc, g_pre.reshape(1, d), g_post.reshape(1, d), wg, wu, wd)


def _proj_kernel(h_ref, modl_ref, modc_ref, gpre_ref, w_ref, wdt_ref, o_ref, odt_ref, n_ref, inv_ref,
                 *, tm, rows, t_lat, base):
    i = pl.program_id(1)
    j = pl.program_id(2)

    @pl.when(j == 0)
    def _():
        _prologue(i, _Rows(h_ref), n_ref, None, inv_ref, gpre_ref, modl_ref, modc_ref, base, tm, rows, t_lat)
        odt_ref[0] = _dot(n_ref[...], wdt_ref[...])

    o_ref[0] = _dot(n_ref[...], w_ref[...])


def in_projection(h, modl, modc, g_pre, w_main, w_dt, *, base, t_lat, tm, tn=512):
    bsz, s, d = h.shape
    ncol = w_main.shape[1]
    ndt = w_dt.shape[1]
    kern = functools.partial(_proj_kernel, tm=tm, rows=s, t_lat=t_lat, base=base)
    return pl.pallas_call(
        kern,
        out_shape=(jax.ShapeDtypeStruct((bsz, s, ncol), F32),
                   jax.ShapeDtypeStruct((bsz, s, ndt), F32)),
        grid=(bsz, s // tm, ncol // tn),
        in_specs=[
            pl.BlockSpec((1, tm, d), lambda b, i, j: (b, i, 0)),
            pl.BlockSpec((1, N_MOD, d), lambda b, i, j: (b, 0, 0)),
            pl.BlockSpec((N_MOD, d), lambda b, i, j: (0, 0)),
            pl.BlockSpec((1, d), lambda b, i, j: (0, 0)),
            pl.BlockSpec((d, tn), lambda b, i, j: (0, j)),
            pl.BlockSpec((d, ndt), lambda b, i, j: (0, 0)),
        ],
        out_specs=(pl.BlockSpec((1, tm, tn), lambda b, i, j: (b, i, j)),
                   pl.BlockSpec((1, tm, ndt), lambda b, i, j: (b, i, 0))),
        scratch_shapes=[pltpu.VMEM((tm, d), BF16), pltpu.VMEM((tm, LANE), F32)],
        compiler_params=_cparams(("parallel", "parallel", "arbitrary")),
        name="in_projection",
    )(h, modl, modc, g_pre.reshape(1, d), w_main, w_dt)


def _merge_kernel(h_ref, modl_ref, modc_ref, gpre_ref, gpost_ref, b0_ref, b1_ref, b2_ref, b3_ref,
                  wg_ref, wb_ref, wo_ref, o_ref, n_ref, inv_ref, *, tm, rows, t_lat, base, nk):
    i = pl.program_id(1)
    k = pl.program_id(2)

    @pl.when(k == 0)
    def _():
        _prologue(i, _Rows(h_ref), n_ref, o_ref, inv_ref, gpre_ref, modl_ref, modc_ref, base, tm, rows, t_lat)

    n = n_ref[...]
    y = None
    for bi, br_ref in enumerate((b0_ref, b1_ref, b2_ref, b3_ref)):
        t = jax.nn.sigmoid(_dot(n, wg_ref[bi])) * _dot(br_ref[0], wb_ref[bi])
        y = t if y is None else y + t
    _accumulate(o_ref, y.astype(BF16), wo_ref, inv_ref)

    @pl.when(k == nk - 1)
    def _():
        _epilogue(i, _Rows(h_ref), o_ref, inv_ref, gpost_ref, modl_ref, modc_ref, base + 2, 1.0, tm, rows, t_lat)


def merge_sublayer(h, modl, modc, g_pre, g_post, branches, wg, wb, wo, *, wl, base, t_lat, rows, tm, tn=256):
    bsz, _, d = h.shape
    w = branches[0].shape[-1]
    nk = d // tn
    kern = functools.partial(_merge_kernel, tm=tm, rows=rows, t_lat=t_lat, base=base, nk=nk)
    br_spec = pl.BlockSpec((1, tm, w), lambda b, i, k: (b, i, 0))
    return pl.pallas_call(
        kern,
        out_shape=jax.ShapeDtypeStruct((bsz, rows, d), F32),
        grid=(bsz, rows // tm, nk),
        in_specs=[
            pl.BlockSpec((1, tm, d), lambda b, i, k: (b, i, 0)),
            pl.BlockSpec((1, N_MOD, d), lambda b, i, k: (b, 0, 0)),
            pl.BlockSpec((N_MOD, d), lambda b, i, k: (0, 0)),
            pl.BlockSpec((1, d), lambda b, i, k: (0, 0)),
            pl.BlockSpec((1, d), lambda b, i, k: (0, 0)),
            br_spec, br_spec, br_spec, br_spec,
            pl.BlockSpec((None, 4, d, tn), lambda b, i, k: (wl, 0, 0, k)),
            pl.BlockSpec((None, 4, w, tn), lambda b, i, k: (wl, 0, 0, k)),
            pl.BlockSpec((None, tn, d), lambda b, i, k: (wl, k, 0)),
        ],
        out_specs=pl.BlockSpec((1, tm, d), lambda b, i, k: (b, i, 0)),
        scratch_shapes=[pltpu.VMEM((tm, d), BF16), pltpu.VMEM((tm, LANE), F32)],
        compiler_params=_cparams(("parallel", "parallel", "arbitrary")),
        name="merge_sublayer",
    )(h, modl, modc, g_pre.reshape(1, d), g_post.reshape(1, d), *branches, wg, wb, wo)


def _dwconv_rows(x, w, b, t_lat):
    s = x.shape[0]
    t = lax.broadcasted_iota(jnp.int32, (s, 1), 0)
    tl = jnp.where(t >= t_lat, t - t_lat, t)
    seg_len = jnp.where(t >= t_lat, s - t_lat, t_lat)
    y = x * w[1:2, :] + b
    for j, d in ((0, -1), (2, 1), (3, 2)):
        xs = pltpu.roll(x, (-d) % s, 0)
        ok = (tl + d >= 0) & (tl + d < seg_len)
        y = y + jnp.where(ok, xs, 0.0) * w[j:j + 1, :]
    return y


def _conv_silu_kernel(x_ref, w_ref, b_ref, o_ref, *, t_lat):
    y = _dwconv_rows(x_ref[0], w_ref[...], b_ref[...], t_lat)
    o_ref[0] = y * jax.nn.sigmoid(y)


def ssm_conv(p, conv_w, conv_b, *, col0, t_lat, tc=512):
    bsz, s, _ = p.shape
    c = conv_w.shape[1]
    off = col0 // tc
    return pl.pallas_call(
        functools.partial(_conv_silu_kernel, t_lat=t_lat),
        out_shape=jax.ShapeDtypeStruct((bsz, s, c), F32),
        grid=(bsz, c // tc),
        in_specs=[
            pl.BlockSpec((1, s, tc), lambda b, j: (b, 0, off + j)),
            pl.BlockSpec((CONV_K, tc), lambda b, j: (0, j)),
            pl.BlockSpec((1, tc), lambda b, j: (0, j)),
        ],
        out_specs=pl.BlockSpec((1, s, tc), lambda b, j: (b, 0, j)),
        compiler_params=_cparams(("parallel", "parallel")),
        name="ssm_conv",
    )(p, conv_w, conv_b.reshape(1, c))


def _ssd_direction(xc, dtraw, a, dtb, expand, state_ref, reverse):
    q = xc.shape[0]
    nh = dtraw.shape[1]
    xw = nh * SSM_HEAD_DIM
    gw = xw // SSM_GROUPS
    hg = nh // SSM_GROUPS
    x = xc[:, :xw]
    dt = jax.nn.softplus(dtraw + dtb)
    d = dt * a
    li = lax.broadcasted_iota(jnp.int32, (q, q), 0)
    si = lax.broadcasted_iota(jnp.int32, (q, q), 1)
    mask = (si >= li) if reverse else (si <= li)
    tri = jnp.where(mask, 1.0, 0.0).astype(BF16)
    tri_t = jnp.where((li >= si) if reverse else (li <= si), 1.0, 0.0).astype(BF16)
    hi = d.astype(BF16)
    mid = (d - hi.astype(F32)).astype(BF16)
    lo = (d - hi.astype(F32) - mid.astype(F32)).astype(BF16)
    cs = _dot(tri, hi) + _dot(tri, mid) + _dot(tri, lo)
    cs_t = _dot_tn(hi, tri_t) + _dot_tn(mid, tri_t) + _dot_tn(lo, tri_t)
    last = 0 if reverse else q - 1
    total = cs[last:last + 1, :]
    w_state = jnp.exp(total - cs) * dt
    e_off = jnp.exp(cs)
    ex = _split_dot(jnp.concatenate([w_state, dt, e_off], axis=0), expand)
    x_state = (x * ex[:q]).astype(BF16)
    x_dt = x * ex[q:2 * q]
    e_off_x = ex[2 * q:]
    lane = lax.broadcasted_iota(jnp.int32, (q, LANE), 1)
    ys = []
    for g in range(SSM_GROUPS):
        bm = xc[:, xw + g * SSM_STATE: xw + (g + 1) * SSM_STATE].astype(BF16)
        cm = xc[:, xw + (SSM_GROUPS + g) * SSM_STATE: xw + (SSM_GROUPS + g + 1) * SSM_STATE].astype(BF16)
        cb = _dot_nt(cm, bm)
        st = state_ref[g]
        y_off = _dot(cm, st.astype(BF16)) * e_off_x[:, g * gw:(g + 1) * gw]
        yd = []
        for hp in range(hg // 2):
            ms = []
            for hh in range(2):
                h = g * hg + hp * 2 + hh
                seg = cs[:, h:h + 1] - cs_t[h:h + 1, :]
                ms.append((cb * jnp.where(mask, jnp.exp(seg), 0.0)).astype(BF16))
            c0 = g * gw + hp * LANE
            xp = x_dt[:, c0:c0 + LANE]
            rhs = jnp.concatenate([jnp.where(lane < SSM_HEAD_DIM, xp, 0.0).astype(BF16),
                                   jnp.where(lane >= SSM_HEAD_DIM, xp, 0.0).astype(BF16)], axis=0)
            yd.append(_dot(jnp.concatenate(ms, axis=1), rhs))
        ys.append(jnp.concatenate(yd, axis=1) + y_off)
        decay = e_off_x[last:last + 1, g * gw:(g + 1) * gw]
        state_ref[g] = decay * st + _dot_tn(bm, x_state[:, g * gw:(g + 1) * gw])
    return jnp.concatenate(ys, axis=1)


def _ssd_kernel(xf_ref, xb_ref, dtf_ref, dtb_ref, a_ref, bias_ref, exp_ref, yf_ref, yb_ref,
                sf_ref, sb_ref, *, nh):
    @pl.when(pl.program_id(1) == 0)
    def _():
        sf_ref[...] = jnp.zeros_like(sf_ref)
        sb_ref[...] = jnp.zeros_like(sb_ref)

    ex = exp_ref[...]
    yf_ref[0] = _ssd_direction(xf_ref[0], dtf_ref[0][:, :nh], a_ref[0:1, :], bias_ref[0:1, :],
                               ex, sf_ref, False)
    yb_ref[0] = _ssd_direction(xb_ref[0], dtb_ref[0][:, nh:2 * nh], a_ref[1:2, :], bias_ref[1:2, :],
                               ex, sb_ref, True)


def ssd_scan(xc, dt_raw, a_log, dt_bias, *, t_lat):
    bsz, s, cw = xc.shape
    nh = a_log.shape[1]
    xw = nh * SSM_HEAD_DIM
    q = SSM_CHUNK
    nt = s // q
    nl = t_lat // q
    nc = nt - nl
    a = -jnp.exp(a_log.astype(F32))
    expand = (jnp.arange(xw)[None, :] // SSM_HEAD_DIM == jnp.arange(nh)[:, None]).astype(BF16)

    def fidx(j):
        return jnp.where(j < nc, nl + j, j - nc)

    def bidx(j):
        return nt - 1 - j

    ndt = dt_raw.shape[-1]
    return pl.pallas_call(
        functools.partial(_ssd_kernel, nh=nh),
        out_shape=(jax.ShapeDtypeStruct((bsz, s, xw), F32), jax.ShapeDtypeStruct((bsz, s, xw), F32)),
        grid=(bsz, nt),
        in_specs=[
            pl.BlockSpec((1, q, cw), lambda b, j: (b, fidx(j), 0)),
            pl.BlockSpec((1, q, cw), lambda b, j: (b, bidx(j), 0)),
            pl.BlockSpec((1, q, ndt), lambda b, j: (b, fidx(j), 0)),
            pl.BlockSpec((1, q, ndt), lambda b, j: (b, bidx(j), 0)),
            pl.BlockSpec((2, nh), lambda b, j: (0, 0)),
            pl.BlockSpec((2, nh), lambda b, j: (0, 0)),
            pl.BlockSpec((nh, xw), lambda b, j: (0, 0)),
        ],
        out_specs=(pl.BlockSpec((1, q, xw), lambda b, j: (b, fidx(j), 0)),
                   pl.BlockSpec((1, q, xw), lambda b, j: (b, bidx(j), 0))),
        scratch_shapes=[pltpu.VMEM((SSM_GROUPS, SSM_STATE, xw // SSM_GROUPS), F32),
                        pltpu.VMEM((SSM_GROUPS, SSM_STATE, xw // SSM_GROUPS), F32)],
        compiler_params=_cparams(("parallel", "arbitrary")),
        name="ssd_scan",
    )(xc, xc, dt_raw, dt_raw, a, dt_bias.astype(F32), expand)


def _ssm_finish_kernel(yf_ref, yb_ref, xs_ref, z_ref, dsk_ref, g_ref, o_ref):
    z = z_ref[0]
    y = (yf_ref[0] + yb_ref[0] + dsk_ref[...] * xs_ref[0]) * (z * jax.nn.sigmoid(z))
    gw = y.shape[1] // SSM_GROUPS
    outs = []
    for g in range(SSM_GROUPS):
        outs.append(_rms(y[:, g * gw:(g + 1) * gw], g_ref[:, g * gw:(g + 1) * gw]))
    o_ref[0] = jnp.concatenate(outs, axis=1).astype(o_ref.dtype)


def ssm_finish(y_f, y_b, xc, p, d_skip, g_norm, *, z_col0, tm):
    bsz, s, xw = y_f.shape
    dsk = jnp.repeat(d_skip.astype(F32), SSM_HEAD_DIM).reshape(1, xw)
    zoff = z_col0 // xw
    spec = pl.BlockSpec((1, tm, xw), lambda b, i: (b, i, 0))
    return pl.pallas_call(
        _ssm_finish_kernel,
        out_shape=jax.ShapeDtypeStruct((bsz, s, xw), BF16),
        grid=(bsz, s // tm),
        in_specs=[spec, spec, spec,
                  pl.BlockSpec((1, tm, xw), lambda b, i: (b, i, zoff)),
                  pl.BlockSpec((1, xw), lambda b, i: (0, 0)),
                  pl.BlockSpec((1, xw), lambda b, i: (0, 0))],
        out_specs=spec,
        compiler_params=_cparams(("parallel", "parallel")),
        name="ssm_finish",
    )(y_f, y_b, xc, p, dsk, g_norm.reshape(1, xw))


def _rope_tables(t_lat, head_dim, reps):
    rows = t_lat // GRID_W
    row = jnp.repeat(jnp.arange(rows, dtype=jnp.int32), GRID_W).astype(F32)
    col = (jnp.arange(t_lat, dtype=jnp.int32) % GRID_W).astype(F32)
    half = head_dim // 2
    inv = ROPE_THETA ** (-(jnp.arange(0, half, 2, dtype=F32) / half))
    ang_r = row[:, None] * inv[None, :]
    ang_c = col[:, None] * inv[None, :]
    cos = jnp.concatenate([jnp.cos(ang_r), jnp.cos(ang_r), jnp.cos(ang_c), jnp.cos(ang_c)], axis=1)
    sin = jnp.concatenate([-jnp.sin(ang_r), jnp.sin(ang_r), -jnp.sin(ang_c), jnp.sin(ang_c)], axis=1)
    return jnp.tile(cos, (1, reps)), jnp.tile(sin, (1, reps))


def _rope(v, cos, sin, quarter):
    w = v.shape[1]
    lane = lax.broadcasted_iota(jnp.int32, v.shape, 1)
    first = (lane % (2 * quarter)) < quarter
    swapped = jnp.where(first, pltpu.roll(v, w - quarter, 1), pltpu.roll(v, quarter, 1))
    return v * cos + swapped * sin


LOG2E = math.log2(math.e)


def _softmax_parts(s2):
    m = jnp.max(s2, axis=-1, keepdims=True)
    e = jnp.exp2(s2 - m)
    return e, jnp.sum(e, axis=-1, keepdims=True)


def _gqa_kernel(q_ref, k_ref, v_ref, cos_ref, sin_ref, cosq_ref, sinq_ref, gq_ref, gk_ref, o_ref,
                kp_ref, vp_ref, *, t_lat, nq_lat, kps):
    qi = pl.program_id(2)
    d = GQA_HEAD_DIM
    quarter = d // 4

    @pl.when(qi == 0)
    def _():
        for h in range(kps):
            cs = slice(h * d, (h + 1) * d)
            kn = _rms(k_ref[0, :, cs], gk_ref[...])
            kp_ref[:t_lat, cs] = _rope(kn[:t_lat], cos_ref[...], sin_ref[...], quarter).astype(BF16)
            kp_ref[t_lat:, cs] = kn[t_lat:].astype(BF16)
            vp_ref[:, cs] = v_ref[0, :, cs].astype(BF16)

    scale = d ** -0.5 * LOG2E

    def attend(rope_q, k0):
        for j in range(kps * GQA_GROUP):
            h = j // GQA_GROUP
            qh = _rms(q_ref[0, :, j * d:(j + 1) * d], gq_ref[...])
            if rope_q:
                qh = _rope(qh, cosq_ref[...], sinq_ref[...], quarter)
            e, l = _softmax_parts(_dot_nt((qh * scale).astype(BF16), kp_ref[k0:, h * d:(h + 1) * d]))
            o = _dot(e.astype(BF16), vp_ref[k0:, h * d:(h + 1) * d]) / l
            o_ref[0, :, j * d:(j + 1) * d] = o.astype(o_ref.dtype)

    @pl.when(qi < nq_lat)
    def _():
        attend(True, 0)

    @pl.when(qi >= nq_lat)
    def _():
        attend(False, t_lat)


def gqa_attention(p, g_q, g_k, *, q_col0, k_col0, v_col0, n_kv, t_lat, with_ctx, tq=256, kps=2):
    bsz, s, _ = p.shape
    d = GQA_HEAD_DIM
    gw = kps * GQA_GROUP * d
    kw = kps * d
    nq_lat = t_lat // tq
    nq = (s if with_ctx else t_lat) // tq
    cos, sin = _rope_tables(t_lat, d, 1)
    qo, ko, vo = q_col0 // gw, k_col0 // kw, v_col0 // kw
    kern = functools.partial(_gqa_kernel, t_lat=t_lat, nq_lat=nq_lat, kps=kps)
    qtab = lambda b, h, i: (jnp.minimum(i, nq_lat - 1), 0)
    return pl.pallas_call(
        kern,
        out_shape=jax.ShapeDtypeStruct((bsz, nq * tq, n_kv * GQA_GROUP * d), BF16),
        grid=(bsz, n_kv // kps, nq),
        in_specs=[
            pl.BlockSpec((1, tq, gw), lambda b, h, i: (b, i, qo + h)),
            pl.BlockSpec((1, s, kw), lambda b, h, i: (b, 0, ko + h)),
            pl.BlockSpec((1, s, kw), lambda b, h, i: (b, 0, vo + h)),
            pl.BlockSpec((t_lat, d), lambda b, h, i: (0, 0)),
            pl.BlockSpec((t_lat, d), lambda b, h, i: (0, 0)),
            pl.BlockSpec((tq, d), qtab),
            pl.BlockSpec((tq, d), qtab),
            pl.BlockSpec((1, d), lambda b, h, i: (0, 0)),
            pl.BlockSpec((1, d), lambda b, h, i: (0, 0)),
        ],
        out_specs=pl.BlockSpec((1, tq, gw), lambda b, h, i: (b, i, h)),
        scratch_shapes=[pltpu.VMEM((s, kw), BF16), pltpu.VMEM((s, kw), BF16)],
        compiler_params=_cparams(("parallel", "parallel", "arbitrary")),
        name="gqa_attention",
    )(p, p, p, cos, sin, cos, sin, g_q.reshape(1, d), g_k.reshape(1, d))


def _diff_kernel(q_ref, k_ref, v_ref, cos_ref, sin_ref, cosq_ref, sinq_ref, lq_ref, lk_ref, g_ref,
                 o_ref, kp_ref, vp_ref, *, t_lat, nq_lat, lambda_init, hps):
    qi = pl.program_id(2)
    d = DIFF_HEAD_DIM
    hw = 2 * d
    quarter = d // 4

    @pl.when(qi == 0)
    def _():
        for j in range(hps):
            cs = slice(j * hw, (j + 1) * hw)
            kp_ref[:t_lat, cs] = _rope(k_ref[0, :t_lat, cs], cos_ref[...], sin_ref[...], quarter).astype(BF16)
            kp_ref[t_lat:, cs] = k_ref[0, t_lat:, cs].astype(BF16)
            vp_ref[:, cs] = v_ref[0, :, cs].astype(BF16)

    prod = lq_ref[...] * lk_ref[...]
    sums = jnp.sum(prod, axis=-1, keepdims=True)
    lam = jnp.exp(sums[0:1]) - jnp.exp(sums[1:2]) + lambda_init
    scale = d ** -0.5 * LOG2E

    def attend(rope_q, k0):
        for j in range(hps):
            cs = slice(j * hw, (j + 1) * hw)
            q = q_ref[0, :, cs]
            if rope_q:
                q = _rope(q, cosq_ref[...], sinq_ref[...], quarter)
            q = q * scale
            lane = lax.broadcasted_iota(jnp.int32, q.shape, 1)
            kk = kp_ref[k0:, cs]
            q0 = jnp.where(lane < d, q, 0.0).astype(BF16)
            q1 = jnp.where(lane >= d, q, 0.0).astype(BF16)
            e0, l0 = _softmax_parts(_dot_nt(q0, kk))
            e1, l1 = _softmax_parts(_dot_nt(q1, kk))
            pd = e0 - (lam * l0 / l1) * e1
            o = _dot(pd.astype(BF16), vp_ref[k0:, cs]) / l0
            o_ref[0, :, cs] = (_rms(o, g_ref[...]) * (1.0 - lambda_init)).astype(o_ref.dtype)

    @pl.when(qi < nq_lat)
    def _():
        attend(True, 0)

    @pl.when(qi >= nq_lat)
    def _():
        attend(False, t_lat)


def diff_attention(p, lambda_q, lambda_k, g_norm, *, q_col0, k_col0, v_col0, n_heads, t_lat,
                   with_ctx, lambda_init, tq=256, hps=4):
    bsz, s, _ = p.shape
    hw = 2 * DIFF_HEAD_DIM
    gw = hps * hw
    nq_lat = t_lat // tq
    nq = (s if with_ctx else t_lat) // tq
    cos, sin = _rope_tables(t_lat, DIFF_HEAD_DIM, 2)
    qo, ko, vo = q_col0 // gw, k_col0 // gw, v_col0 // gw
    kern = functools.partial(_diff_kernel, t_lat=t_lat, nq_lat=nq_lat, lambda_init=lambda_init, hps=hps)
    qtab = lambda b, h, i: (jnp.minimum(i, nq_lat - 1), 0)
    return pl.pallas_call(
        kern,
        out_shape=jax.ShapeDtypeStruct((bsz, nq * tq, n_heads * hw), BF16),
        grid=(bsz, n_heads // hps, nq),
        in_specs=[
            pl.BlockSpec((1, tq, gw), lambda b, h, i: (b, i, qo + h)),
            pl.BlockSpec((1, s, gw), lambda b, h, i: (b, 0, ko + h)),
            pl.BlockSpec((1, s, gw), lambda b, h, i: (b, 0, vo + h)),
            pl.BlockSpec((t_lat, hw), lambda b, h, i: (0, 0)),
            pl.BlockSpec((t_lat, hw), lambda b, h, i: (0, 0)),
            pl.BlockSpec((tq, hw), qtab),
            pl.BlockSpec((tq, hw), qtab),
            pl.BlockSpec((2, DIFF_HEAD_DIM), lambda b, h, i: (0, 0)),
            pl.BlockSpec((2, DIFF_HEAD_DIM), lambda b, h, i: (0, 0)),
            pl.BlockSpec((1, hw), lambda b, h, i: (0, 0)),
        ],
        out_specs=pl.BlockSpec((1, tq, gw), lambda b, h, i: (b, i, h)),
        scratch_shapes=[pltpu.VMEM((s, gw), BF16), pltpu.VMEM((s, gw), BF16)],
        compiler_params=_cparams(("parallel", "parallel", "arbitrary")),
        name="diff_attention",
    )(p, p, p, cos, sin, cos, sin, lambda_q.astype(F32), lambda_k.astype(F32), g_norm.reshape(1, hw))


def _scan_rows(a_ref, b_ref, h_ref, row0, nblk, carry, reverse, accumulate):
    rows = lax.broadcasted_iota(jnp.int32, (SUBLANE, a_ref.shape[1]), 0)

    def body(i, carry):
        blk = (nblk - 1 - i) if reverse else i
        r0 = pl.multiple_of(row0 + blk * SUBLANE, SUBLANE)
        a = a_ref[pl.ds(r0, SUBLANE), :]
        b = b_ref[pl.ds(r0, SUBLANE), :]
        for sft in (1, 2, 4):
            if reverse:
                a_s = pltpu.roll(a, SUBLANE - sft, 0)
                b_s = pltpu.roll(b, SUBLANE - sft, 0)
                ok = rows < SUBLANE - sft
            else:
                a_s = pltpu.roll(a, sft, 0)
                b_s = pltpu.roll(b, sft, 0)
                ok = rows >= sft
            b = jnp.where(ok, a * b_s + b, b)
            a = jnp.where(ok, a * a_s, a)
        h = b + a * carry
        if accumulate:
            h_ref[pl.ds(r0, SUBLANE), :] += h
        else:
            h_ref[pl.ds(r0, SUBLANE), :] = h
        return h[0:1, :] if reverse else h[SUBLANE - 1:SUBLANE, :]

    return lax.fori_loop(0, nblk, body, carry, unroll=4)


LRU_ROW_CHUNK = 256


def _sigmoid_tanh(x):
    return 0.5 * jnp.tanh(0.5 * x) + 0.5


def _lru_kernel(x_ref, g_ref, cw_ref, cb_ref, wrg_ref, wig_ref, brg_ref, big_ref, lam_ref, o_ref,
                xr_ref, a_ref, b_ref, h_ref, *, t_lat):
    s, w = x_ref.shape[1], x_ref.shape[2]
    xr_ref[...] = _dwconv_rows(x_ref[0], cw_ref[...], cb_ref[...], t_lat)
    n_lat = t_lat // SUBLANE
    n_ctx = (s - t_lat) // SUBLANE
    zero = jnp.zeros((1, w), F32)
    for dr in range(2):
        sp = jax.nn.softplus(-lam_ref[dr:dr + 1, :])

        def gates(r, carry, dr=dr, sp=sp):
            rows = pl.ds(pl.multiple_of(r * LRU_ROW_CHUNK, LRU_ROW_CHUNK), LRU_ROW_CHUNK)
            for c in range(w // LANE):
                cs = slice(c * LANE, (c + 1) * LANE)
                xr = xr_ref[rows, cs]
                xb = xr.astype(BF16)
                rg = _sigmoid_tanh(_dot(xb, wrg_ref[dr, c]) + brg_ref[dr:dr + 1, cs])
                ig = _sigmoid_tanh(_dot(xb, wig_ref[dr, c]) + big_ref[dr:dr + 1, cs])
                log_a = -LRU_C * rg * sp[:, cs]
                a = jnp.exp(log_a)
                a_ref[rows, cs] = a
                b_ref[rows, cs] = jnp.sqrt(-jnp.tanh(log_a) * (a * a + 1.0)) * (ig * xr)
            return carry

        lax.fori_loop(0, s // LRU_ROW_CHUNK, gates, 0)
        if dr == 0:
            c = _scan_rows(a_ref, b_ref, h_ref, t_lat, n_ctx, zero, False, False)
            _scan_rows(a_ref, b_ref, h_ref, 0, n_lat, c, False, False)
        else:
            c = _scan_rows(a_ref, b_ref, h_ref, t_lat, n_ctx, zero, True, True)
            _scan_rows(a_ref, b_ref, h_ref, 0, n_lat, c, True, True)

    def finish(r, carry):
        rows = pl.ds(pl.multiple_of(r * LRU_ROW_CHUNK, LRU_ROW_CHUNK), LRU_ROW_CHUNK)
        o_ref[0, rows, :] = (jax.nn.gelu(g_ref[0, rows, :]) * h_ref[rows, :]).astype(o_ref.dtype)
        return carry

    lax.fori_loop(0, s // LRU_ROW_CHUNK, finish, 0)


def rglru(p, conv_w, conv_b, w_rg, b_rg, w_ig, b_ig, lam, *, g_col0, x_col0, t_lat, cw=512):
    bsz, s, _ = p.shape
    width = conv_w.shape[1]
    ncg = width // cw
    nsub = cw // LANE
    go, xo = g_col0 // cw, x_col0 // cw

    def pair_blocks(w):
        nb = w.shape[1]
        w = w.reshape(2, nb // 2, 2, LRU_BLOCK_DIM, LRU_BLOCK_DIM)
        z = jnp.zeros_like(w[:, :, 0])
        top = jnp.concatenate([w[:, :, 0], z], axis=-1)
        bot = jnp.concatenate([z, w[:, :, 1]], axis=-1)
        return jnp.concatenate([top, bot], axis=-2).astype(BF16)

    wspec = pl.BlockSpec((2, nsub, LANE, LANE), lambda b, c: (0, c, 0, 0))
    vspec = pl.BlockSpec((2, cw), lambda b, c: (0, c))
    return pl.pallas_call(
        functools.partial(_lru_kernel, t_lat=t_lat),
        out_shape=jax.ShapeDtypeStruct((bsz, s, width), BF16),
        grid=(bsz, ncg),
        in_specs=[
            pl.BlockSpec((1, s, cw), lambda b, c: (b, 0, xo + c)),
            pl.BlockSpec((1, s, cw), lambda b, c: (b, 0, go + c)),
            pl.BlockSpec((CONV_K, cw), lambda b, c: (0, c)),
            pl.BlockSpec((1, cw), lambda b, c: (0, c)),
            wspec, wspec, vspec, vspec, vspec,
        ],
        out_specs=pl.BlockSpec((1, s, cw), lambda b, c: (b, 0, c)),
        scratch_shapes=[pltpu.VMEM((s, cw), F32)] * 4,
        compiler_params=_cparams(("parallel", "parallel")),
        name="rglru",
    )(p, p, conv_w.astype(F32), conv_b.reshape(1, width).astype(F32), pair_blocks(w_rg), pair_blocks(w_ig),
      b_rg.astype(F32), b_ig.astype(F32), lam.astype(F32))


def _split_w_in(w_in, mix_w, n_ssm_heads):
    gn = SSM_GROUPS * SSM_STATE
    kv = mix_w // GQA_GROUP
    sizes = (mix_w, mix_w + 2 * gn, 2 * n_ssm_heads, mix_w, mix_w, mix_w, mix_w, kv, kv, mix_w, mix_w)
    names = ("z", "xbc", "dt", "q_d", "k_d", "v_d", "q_g", "k_g", "v_g", "g_lru", "x_lru")
    parts = {}
    o = 0
    for nme, sz in zip(names, sizes):
        parts[nme] = w_in[:, o:o + sz]
        o += sz
    order = ("q_g", "q_d", "k_d", "v_d", "z", "g_lru", "x_lru", "xbc", "k_g", "v_g")
    offs = {}
    o = 0
    for nme in order:
        offs[nme] = o
        o += parts[nme].shape[1]
    w_main = jnp.concatenate([parts[nme] for nme in order], axis=1).astype(BF16)
    w_dt = jnp.pad(parts["dt"], ((0, 0), (0, LANE - 2 * n_ssm_heads))).astype(BF16)
    return w_main, w_dt, offs


def kernel(x, c, ctx, c_ctx, w_ada, b_ada, g_pre, g_post, w_ffn_gate, w_ffn_up, w_ffn_down, w_in, conv_w_ssm, conv_b_ssm, a_log, dt_bias, d_skip, g_ssm_norm, lambda_q, lambda_k, g_diff_norm, g_q_norm, g_k_norm, conv_w_lru, conv_b_lru, w_rg, b_rg, w_ig, b_ig, lru_lambda, w_branch, w_gate, w_out):
    bsz, t_lat, d = x.shape
    depth = w_ada.shape[0]
    mix_w = d // 2
    n_ssm_heads = a_log.shape[-1]
    s = t_lat + ctx.shape[1]
    tiles = {"ffn": s // 3, "proj": s // 2, "merge": s // 4, "finish": s // 4, "lat": t_lat // 4}
    assert all(t % ROW_CHUNK == 0 for t in tiles.values()) and t_lat % ROW_CHUNK == 0, tiles

    r = ((bsz + 1 + 15) // 16) * 16
    cc = jnp.zeros((r, d), F32).at[:bsz].set(c).at[bsz].set(c_ctx)
    mods = adaln(cc, w_ada, b_ada)

    wg_all, wu_all, wd_all = (w.astype(BF16) for w in (w_ffn_gate, w_ffn_up, w_ffn_down))
    wgate_all, wbranch_all, wout_all = (w.astype(BF16) for w in (w_gate, w_branch, w_out))

    h = x
    for l in range(depth):
        last = l == depth - 1
        lambda_init = 0.8 - 0.6 * math.exp(-0.3 * l)
        modl = mods[l, :bsz].reshape(bsz, N_MOD, d)
        modc = mods[l, bsz].reshape(N_MOD, d)
        w_main, w_dt, offs = _split_w_in(w_in[l], mix_w, n_ssm_heads)

        h = ffn_sublayer(h, ctx, modl, modc, g_pre[l, 0], g_post[l, 0], wg_all, wu_all, wd_all,
                         widx=(l, 0), base=0, t_lat=t_lat, rows=s, tm=tiles["ffn"])

        p, dt_raw = in_projection(h, modl, modc, g_pre[l, 1], w_main, w_dt, base=3, t_lat=t_lat,
                                  tm=tiles["proj"], tn=1024)

        xc = ssm_conv(p, conv_w_ssm[l].astype(F32), conv_b_ssm[l].astype(F32), col0=offs["xbc"], t_lat=t_lat)
        y_f, y_b = ssd_scan(xc, dt_raw, a_log[l], dt_bias[l], t_lat=t_lat)
        ssm = ssm_finish(y_f, y_b, xc, p, d_skip[l], g_ssm_norm[l], z_col0=offs["z"], tm=tiles["finish"])

        dif = diff_attention(p, lambda_q[l], lambda_k[l], g_diff_norm[l], q_col0=offs["q_d"],
                             k_col0=offs["k_d"], v_col0=offs["v_d"], n_heads=mix_w // (2 * DIFF_HEAD_DIM),
                             t_lat=t_lat, with_ctx=not last, lambda_init=lambda_init)
        gqa = gqa_attention(p, g_q_norm[l], g_k_norm[l], q_col0=offs["q_g"], k_col0=offs["k_g"],
                            v_col0=offs["v_g"], n_kv=mix_w // (GQA_GROUP * GQA_HEAD_DIM), t_lat=t_lat,
                            with_ctx=not last)
        lru = rglru(p, conv_w_lru[l], conv_b_lru[l], w_rg[l], b_rg[l], w_ig[l], b_ig[l], lru_lambda[l],
                    g_col0=offs["g_lru"], x_col0=offs["x_lru"], t_lat=t_lat)

        rows = t_lat if last else s
        h = merge_sublayer(h, modl, modc, g_pre[l, 1], g_post[l, 1], (ssm, dif, gqa, lru),
                           wgate_all, wbranch_all, wout_all, wl=l,
                           base=3, t_lat=t_lat, rows=rows, tm=tiles["lat" if last else "merge"])
        h = ffn_sublayer(h, ctx, modl, modc, g_pre[l, 2], g_post[l, 2], wg_all, wu_all, wd_all,
                         widx=(l, 1), base=6, t_lat=t_lat, rows=rows, tm=tiles["lat" if last else "ffn"])
    return h
```
